```python
import numpy as np
import jax, jax.numpy as jnp
from jax import lax

D_MODEL = 1024
BATCH = 4
SEQ = 4096
DEPTH = 4
DEC_BATCH = 32
DEC_SEQ = 8
PAST_LEN = 8192
PAGE_SIZE = 128

D_MIX = D_MODEL
HEAD_DIM = 64
D_ATTN = D_MIX // 2
N_HEADS = D_ATTN // HEAD_DIM
D_CONV = D_MIX // 4
D_POOL = D_MIX - D_ATTN - D_CONV
CONV_WIDTH = 3
POOL_WINDOWS = (2, 4, 8, 16)
N_POOL_GROUPS = 4
POOL_GROUP = D_POOL // N_POOL_GROUPS
POOL_CTX = 15
Q_BLOCK = 128
RMS_EPS = 1e-6
SB_BIAS_INIT = -7.0
IN_SIZES = (D_CONV, D_CONV, D_CONV, D_CONV, D_ATTN, D_ATTN, D_ATTN, D_ATTN, D_POOL, D_POOL)
D_IN = 4 * D_CONV + 4 * D_ATTN + 2 * D_POOL

kernel_name = "hybrid_conv_stickbreak_pool_decoder_step"


def _rmsnorm(x, w):
    xf = x.astype(jnp.float32)
    xf = xf * lax.rsqrt(jnp.mean(xf * xf, axis=-1, keepdims=True) + RMS_EPS)
    return (xf * w.astype(jnp.float32)).astype(x.dtype)


def _sb_block(q, k, v, sb_bias, q_pos, k_pos):
    z = jnp.einsum('bqhd,bshd->bhqs', q.astype(jnp.float32), k.astype(jnp.float32)) * (HEAD_DIM ** -0.5)
    z = z + sb_bias.astype(jnp.float32)[None, :, None, None]
    mask = k_pos[None, :] < q_pos[:, None]
    log_fail = jnp.where(mask, jax.nn.log_sigmoid(-z), 0.0)
    shifted = jnp.pad(log_fail[..., 1:], ((0, 0), (0, 0), (0, 0), (0, 1)))
    after = lax.cumsum(shifted, axis=3, reverse=True)
    w = jnp.where(mask, jnp.exp(jax.nn.log_sigmoid(z) + after), 0.0)
    o = jnp.einsum('bhqs,bshd->bqhd', w, v.astype(jnp.float32))
    return o.astype(q.dtype)


def _sb_attention(q, k, v, sb_bias, q_pos, k_pos):
    b, t, h, d = q.shape
    if t <= Q_BLOCK or t % Q_BLOCK != 0:
        return _sb_block(q, k, v, sb_bias, q_pos, k_pos)
    nb = t // Q_BLOCK
    qb = q.reshape(b, nb, Q_BLOCK, h, d).transpose(1, 0, 2, 3, 4)
    pb = q_pos.reshape(nb, Q_BLOCK)
    ob = lax.map(lambda a: _sb_block(a[0], k, v, sb_bias, a[1], k_pos), (qb, pb))
    return ob.transpose(1, 0, 2, 3, 4).reshape(b, t, h, d)


def _mixer_layer(x, c, conv_prev, pool_prev, past_k, past_v,
                 norm_w, w_ada, b_ada, w_in, sb_bias, conv_w, pool_w, pool_scale, w_out):
    bsz, t, _ = x.shape
    pos0 = 0 if past_k is None else past_k.shape[1]
    pos = pos0 + jnp.arange(t, dtype=jnp.int32)

    ada = jax.nn.silu(c) @ w_ada + b_ada
    shift, scale, gate = jnp.split(ada, 3, axis=-1)
    h = _rmsnorm(x, norm_w) * (1 + scale[:, None, :]) + shift[:, None, :]

    proj = h @ w_in
    idx = np.cumsum(IN_SIZES)[:-1].tolist()
    a_b, a_c, a_h, a_z, q, k, v, b_z, p_u, p_z = jnp.split(proj, idx, axis=-1)

    conv_in = a_c * a_h
    ext = jnp.concatenate([conv_prev.astype(conv_in.dtype), conv_in], axis=1)
    conv_out = ext[:, 0:t] * conv_w[0]
    for j in range(1, CONV_WIDTH):
        conv_out = conv_out + ext[:, j:j + t] * conv_w[j]
    y_a = a_b * conv_out * jax.nn.silu(a_z)
    new_conv = ext[:, t:]

    qh = q.reshape(bsz, t, N_HEADS, HEAD_DIM)
    kh = k.reshape(bsz, t, N_HEADS, HEAD_DIM)
    vh = v.reshape(bsz, t, N_HEADS, HEAD_DIM)
    if past_k is None:
        k_all, v_all, k_pos = kh, vh, pos
    else:
        k_all = jnp.concatenate([past_k.astype(kh.dtype), kh], axis=1)
        v_all = jnp.concatenate([past_v.astype(vh.dtype), vh], axis=1)
        k_pos = jnp.arange(pos0 + t, dtype=jnp.int32)
    o = _sb_attention(qh, k_all, v_all, sb_bias, pos, k_pos)
    y_b = o.reshape(bsz, t, D_ATTN) * jax.nn.silu(b_z)

    ext_u = jnp.concatenate([pool_prev.astype(p_u.dtype), p_u], axis=1)
    cs = jnp.pad(jnp.cumsum(ext_u.astype(jnp.float32), axis=1), ((0, 0), (1, 0), (0, 0)))
    means = []
    for g, win in enumerate(POOL_WINDOWS):
        sl = slice(g * POOL_GROUP, (g + 1) * POOL_GROUP)
        s = cs[:, POOL_CTX + 1:POOL_CTX + 1 + t, sl] - cs[:, POOL_CTX + 1 - win:POOL_CTX + 1 + t - win, sl]
        cnt = jnp.minimum(win, pos + 1).astype(jnp.float32)
        means.append(s / cnt[None, :, None])
    pooled = jnp.concatenate(means, axis=-1) - p_u.astype(jnp.float32)
    pooled = pooled.astype(x.dtype).reshape(bsz, t, N_POOL_GROUPS, POOL_GROUP)
    y_c = jnp.einsum('btgc,gcd->btgd', pooled, pool_w).reshape(bsz, t, D_POOL)
    y_c = y_c * pool_scale * jax.nn.silu(p_z)
    new_pool = ext_u[:, t:]

    out = jnp.concatenate([y_a, y_b, y_c], axis=-1) @ w_out
    x = x + gate[:, None, :] * out
    return x, kh, vh, new_conv, new_pool


def setup_inputs(seed: int = 0) -> dict:
    key = jax.random.key(seed)
    ks = jax.random.split(key, 20)
    f32 = jnp.float32
    n_pages = PAST_LEN // PAGE_SIZE
    n_used = DEC_BATCH * n_pages
    n_pool = n_used + n_used // 4
    page_table = jax.random.permutation(ks[0], n_pool)[:n_used].reshape(DEC_BATCH, n_pages).astype(jnp.int32)
    nrm = jax.random.normal
    return {
        "x_prompt": nrm(ks[1], (BATCH, SEQ, D_MODEL), f32),
        "x_sample": nrm(ks[2], (DEC_BATCH, DEC_SEQ, D_MODEL), f32),
        "cache_k": nrm(ks[3], (DEPTH, n_pool, PAGE_SIZE, N_HEADS, HEAD_DIM), f32),
        "cache_v": nrm(ks[4], (DEPTH, n_pool, PAGE_SIZE, N_HEADS, HEAD_DIM), f32),
        "state_conv": nrm(ks[5], (DEPTH, DEC_BATCH, CONV_WIDTH - 1, D_CONV), f32),
        "state_pool": nrm(ks[6], (DEPTH, DEC_BATCH, POOL_CTX, D_POOL), f32),
        "page_table": page_table,
        "c_prompt": nrm(ks[7], (BATCH, D_MODEL), f32),
        "c_sample": nrm(ks[8], (DEC_BATCH, D_MODEL), f32),
        "norm_w": 1.0 + 0.1 * nrm(ks[9], (DEPTH, D_MODEL), f32),
        "w_ada": nrm(ks[10], (DEPTH, D_MODEL, 3 * D_MODEL), f32) * D_MODEL ** -0.5,
        "b_ada": 0.01 * nrm(ks[11], (DEPTH, 3 * D_MODEL), f32),
        "w_in": nrm(ks[12], (DEPTH, D_MODEL, D_IN), f32) * D_MODEL ** -0.5,
        "sb_bias": SB_BIAS_INIT + 0.1 * nrm(ks[18], (DEPTH, N_HEADS), f32),
        "conv_w": nrm(ks[13], (DEPTH, CONV_WIDTH, D_CONV), f32) * CONV_WIDTH ** -0.5,
        "pool_w": nrm(ks[14], (DEPTH, N_POOL_GROUPS, POOL_GROUP, POOL_GROUP), f32) * POOL_GROUP ** -0.5,
        "pool_scale": 1.0 + 0.1 * nrm(ks[15], (DEPTH, D_POOL), f32),
        "w_out": nrm(ks[16], (DEPTH, D_MIX, D_MODEL), f32) * D_MIX ** -0.5,
        "final_norm_w": 1.0 + 0.1 * nrm(ks[17], (D_MODEL,), f32),
    }


def reference(x_prompt, x_sample, cache_k, cache_v, state_conv, state_pool, page_table,
              c_prompt, c_sample, norm_w, w_ada, b_ada, w_in, sb_bias, conv_w, pool_w, pool_scale,
              w_out, final_norm_w):
    dec_b, n_pages = page_table.shape
    past_len = n_pages * cache_k.shape[2]
    b = x_prompt.shape[0]
    zero_conv = jnp.zeros((b, CONV_WIDTH - 1, D_CONV), x_prompt.dtype)
    zero_pool = jnp.zeros((b, POOL_CTX, D_POOL), x_prompt.dtype)
    xp, xs = x_prompt, x_sample
    kp_l, vp_l, cp_l, pp_l = [], [], [], []
    ks_l, vs_l, cs_l, ps_l = [], [], [], []
    for l in range(DEPTH):
        w = (norm_w[l], w_ada[l], b_ada[l], w_in[l], sb_bias[l], conv_w[l], pool_w[l], pool_scale[l], w_out[l])
        xp, kp, vp, cp, pp = _mixer_layer(xp, c_prompt, zero_conv, zero_pool, None, None, *w)
        past_k = cache_k[l][page_table].reshape(dec_b, past_len, N_HEADS, HEAD_DIM)
        past_v = cache_v[l][page_table].reshape(dec_b, past_len, N_HEADS, HEAD_DIM)
        xs, k_s, v_s, c_s, p_s = _mixer_layer(xs, c_sample, state_conv[l], state_pool[l], past_k, past_v, *w)
        kp_l.append(kp); vp_l.append(vp); cp_l.append(cp); pp_l.append(pp)
        ks_l.append(k_s); vs_l.append(v_s); cs_l.append(c_s); ps_l.append(p_s)
    y_prompt = _rmsnorm(xp, final_norm_w)
    y_sample = _rmsnorm(xs, final_norm_w)
    return (y_prompt, y_sample,
            jnp.stack(kp_l), jnp.stack(vp_l), jnp.stack(cp_l), jnp.stack(pp_l),
            jnp.stack(ks_l), jnp.stack(vs_l), jnp.stack(cs_l), jnp.stack(ps_l))
```

```python
import functools

import jax
import jax.numpy as jnp
from jax import lax
from jax.experimental import pallas as pl
from jax.experimental.pallas import tpu as pltpu

D_MODEL = 1024
HEAD_DIM = 64
D_ATTN = 512
N_HEADS = 8
D_CONV = 256
D_POOL = 256
CONV_WIDTH = 3
POOL_WINDOWS = (2, 4, 8, 16)
POOL_GROUP = 64
POOL_CTX = 15
RMS_EPS = 1e-6
D_IN = 4 * D_CONV + 4 * D_ATTN + 2 * D_POOL
ATTN_COL0 = 4 * D_CONV
POOL_COL0 = ATTN_COL0 + 4 * D_ATTN

LANES = 128
SUBLANES = 8
CONV_HALO = SUBLANES
POOL_HALO = 16
ROW_TILE = 512
Q_TILE = 256
VMEM_LIMIT = 52 * 1024 * 1024

_BF16 = jnp.bfloat16
_F32 = jnp.float32


def _silu(x):
    return x * jax.nn.sigmoid(x)


def _softplus(z):
    return jnp.maximum(z, 0.0) + jnp.log1p(jnp.exp(-jnp.abs(z)))


def _split_bf16(x):
    hi = x.astype(_BF16)
    lo = (x - hi.astype(_F32)).astype(_BF16)
    return hi, lo


def _ada_kernel(c_ref, w_ref, b_ref, o_ref):
    a = _silu(c_ref[...]).astype(_BF16)
    w = w_ref[0].astype(_BF16)
    o_ref[0] = jnp.dot(a, w, preferred_element_type=_F32) + b_ref[0]


def _ada_call(c_all, w_ada, b_ada):
    depth = w_ada.shape[0]
    n = c_all.shape[0]
    return pl.pallas_call(
        _ada_kernel,
        grid=(depth, 3),
        in_specs=[
            pl.BlockSpec((n, D_MODEL), lambda l, j: (0, 0)),
            pl.BlockSpec((1, D_MODEL, D_MODEL), lambda l, j: (l, 0, j)),
            pl.BlockSpec((1, 1, D_MODEL), lambda l, j: (l, 0, j)),
        ],
        out_specs=pl.BlockSpec((1, n, D_MODEL), lambda l, j: (l, 0, j)),
        out_shape=jax.ShapeDtypeStruct((depth, n, 3 * D_MODEL), _F32),
        compiler_params=pltpu.CompilerParams(
            dimension_semantics=("arbitrary", "arbitrary"),
            vmem_limit_bytes=VMEM_LIMIT),
        name="ada",
    )(c_all, w_ada, b_ada.reshape(depth, 1, 3 * D_MODEL))


def _inproj_kernel(x_ref, shift_ref, scale_ref, normw_ref, w_ref, convprev_ref, poolprev_ref,
                   convw_ref, poolw_ref, pscale_ref,
                   ya_ref, yc_ref, gateb_ref, qb_ref, kb_ref, vb_ref, k_ref, v_ref,
                   newconv_ref, newpool_ref, extc_ref, extu_ref, *, nb, tm, n_tiles, pos0):
    i = pl.program_id(1)
    rows = nb * tm

    x = x_ref[...]
    ms = jnp.mean(x * x, axis=-1, keepdims=True)
    xn = x * lax.rsqrt(ms + RMS_EPS) * normw_ref[...]
    h = xn * (1.0 + scale_ref[...]) + shift_ref[...]
    h2 = h.reshape(rows, D_MODEL).astype(_BF16)

    @pl.when(i == 0)
    def _():
        extc_ref[:, CONV_HALO - 2:CONV_HALO, :] = convprev_ref[...]
        extu_ref[:, POOL_HALO - POOL_CTX:POOL_HALO, :] = poolprev_ref[...]

    pc = jnp.dot(h2, w_ref[:, 0:ATTN_COL0], preferred_element_type=_F32)
    a_b = pc[:, 0:D_CONV].reshape(nb, tm, D_CONV)
    a_c = pc[:, D_CONV:2 * D_CONV]
    a_h = pc[:, 2 * D_CONV:3 * D_CONV]
    a_z = pc[:, 3 * D_CONV:4 * D_CONV].reshape(nb, tm, D_CONV)
    conv_in = (a_c * a_h).reshape(nb, tm, D_CONV)
    extc_ref[:, CONV_HALO:CONV_HALO + tm, :] = conv_in
    cw = convw_ref[...]
    conv_out = (extc_ref[:, CONV_HALO - 2:CONV_HALO - 2 + tm, :] * cw[0:1]
                + extc_ref[:, CONV_HALO - 1:CONV_HALO - 1 + tm, :] * cw[1:2]
                + conv_in * cw[2:3])
    ya_ref[...] = (a_b * conv_out * _silu(a_z)).astype(_BF16)
    last_conv = extc_ref[:, CONV_HALO + tm - 2:CONV_HALO + tm, :]
    newconv_ref[...] = last_conv
    if n_tiles > 1:
        extc_ref[:, CONV_HALO - 2:CONV_HALO, :] = last_conv

    pa = jnp.dot(h2, w_ref[:, ATTN_COL0:POOL_COL0], preferred_element_type=_F32)
    q = pa[:, 0:D_ATTN].reshape(nb, tm, D_ATTN)
    k = pa[:, D_ATTN:2 * D_ATTN].reshape(nb, tm, D_ATTN)
    v = pa[:, 2 * D_ATTN:3 * D_ATTN].reshape(nb, tm, D_ATTN)
    b_z = pa[:, 3 * D_ATTN:4 * D_ATTN].reshape(nb, tm, D_ATTN)
    qb_ref[...] = (q * (HEAD_DIM ** -0.5)).astype(_BF16)
    k_ref[...] = k
    v_ref[...] = v
    kb_ref[...] = k.astype(_BF16)
    vb_ref[...] = v.astype(_BF16)
    gateb_ref[...] = _silu(b_z)

    pp = jnp.dot(h2, w_ref[:, POOL_COL0:D_IN], preferred_element_type=_F32)
    p_u = pp[:, 0:D_POOL].reshape(nb, tm, D_POOL)
    p_z = pp[:, D_POOL:2 * D_POOL].reshape(nb, tm, D_POOL)
    extu_ref[:, POOL_HALO:POOL_HALO + tm, :] = p_u

    def win_sum(lo, hi, lane0):
        acc = extu_ref[:, POOL_HALO - lo:POOL_HALO - lo + tm, lane0:lane0 + LANES]
        for kk in range(lo + 1, hi):
            acc = acc + extu_ref[:, POOL_HALO - kk:POOL_HALO - kk + tm, lane0:lane0 + LANES]
        return acc

    s2 = win_sum(0, 2, 0)
    s4 = s2 + win_sum(2, 4, 0)
    s8 = win_sum(0, 8, LANES)
    s16 = s8 + win_sum(8, 16, LANES)
    pos = pos0 + i * tm + lax.broadcasted_iota(jnp.int32, (nb, tm, LANES), 1)
    lane = lax.broadcasted_iota(jnp.int32, (nb, tm, LANES), 2)
    first = lane < POOL_GROUP

    def cnt(win):
        return jnp.minimum(win, pos + 1).astype(_F32)

    mean_lo = jnp.where(first, s2 / cnt(2), s4 / cnt(4))
    mean_hi = jnp.where(first, s8 / cnt(8), s16 / cnt(16))
    pooled = jnp.concatenate([mean_lo, mean_hi], axis=-1) - p_u
    y_c = jnp.dot(pooled.reshape(rows, D_POOL).astype(_BF16), poolw_ref[...],
                  preferred_element_type=_F32).reshape(nb, tm, D_POOL)
    yc_ref[...] = (y_c * pscale_ref[...] * _silu(p_z)).astype(_BF16)
    last_pool = extu_ref[:, POOL_HALO + tm - POOL_CTX:POOL_HALO + tm, :]
    newpool_ref[...] = last_pool
    if n_tiles > 1:
        extu_ref[:, POOL_HALO - POOL_CTX:POOL_HALO, :] = last_pool


def _inproj_call(x, ada3, norm_w, w_in_bf, conv_prev, pool_prev, conv_w, pool_w_bd, pool_scale,
                 *, nb, tm, pos0):
    n_seq, t, _ = x.shape
    n_tiles = t // tm
    grid = (n_seq // nb, n_tiles)
    row_spec = lambda width: pl.BlockSpec((nb, tm, width), lambda b, i: (b, i, 0))
    ada_spec = lambda j: pl.BlockSpec((nb, 1, D_MODEL), lambda b, i, j=j: (b, 0, j))
    const2 = lambda shape: pl.BlockSpec(shape, lambda b, i: (0, 0))
    state_spec = lambda r, c: pl.BlockSpec((nb, r, c), lambda b, i: (b, 0, 0))
    sds = jax.ShapeDtypeStruct
    out_shape = (
        sds((n_seq, t, D_CONV), _BF16), sds((n_seq, t, D_POOL), _BF16),
        sds((n_seq, t, D_ATTN), _F32),
        sds((n_seq, t, D_ATTN), _BF16), sds((n_seq, t, D_ATTN), _BF16), sds((n_seq, t, D_ATTN), _BF16),
        sds((n_seq, t, D_ATTN), _F32), sds((n_seq, t, D_ATTN), _F32),
        sds((n_seq, CONV_WIDTH - 1, D_CONV), _F32), sds((n_seq, POOL_CTX, D_POOL), _F32),
    )
    out_specs = (
        row_spec(D_CONV), row_spec(D_POOL), row_spec(D_ATTN),
        row_spec(D_ATTN), row_spec(D_ATTN), row_spec(D_ATTN),
        row_spec(D_ATTN), row_spec(D_ATTN),
        state_spec(CONV_WIDTH - 1, D_CONV), state_spec(POOL_CTX, D_POOL),
    )
    return pl.pallas_call(
        functools.partial(_inproj_kernel, nb=nb, tm=tm, n_tiles=n_tiles, pos0=pos0),
        grid=grid,
        in_specs=[
            row_spec(D_MODEL), ada_spec(0), ada_spec(1),
            const2((1, D_MODEL)), const2((D_MODEL, D_IN)),
            state_spec(CONV_WIDTH - 1, D_CONV), state_spec(POOL_CTX, D_POOL),
            const2((CONV_WIDTH, D_CONV)), const2((D_POOL, D_POOL)), const2((1, D_POOL)),
        ],
        out_specs=out_specs,
        out_shape=out_shape,
        scratch_shapes=[
            pltpu.VMEM((nb, CONV_HALO + tm, D_CONV), _F32),
            pltpu.VMEM((nb, POOL_HALO + tm, D_POOL), _F32),
        ],
        compiler_params=pltpu.CompilerParams(
            dimension_semantics=("arbitrary", "arbitrary"),
            vmem_limit_bytes=VMEM_LIMIT),
        name="inproj",
    )(x, ada3, ada3, norm_w.reshape(1, D_MODEL), w_in_bf, conv_prev, pool_prev,
      conv_w, pool_w_bd, pool_scale.reshape(1, D_POOL))


def _prompt_attn_kernel(bias_ref, q_ref, k_ref, v_ref, gate_ref, tri_ref, o_ref,
                        carry_ref, acc_ref, *, tq):
    p = pl.program_id(1)
    qi = pl.program_id(2)
    q2 = q_ref[0]
    lane = lax.broadcasted_iota(jnp.int32, (tq, LANES), 1)
    zero = jnp.zeros_like(q2)
    qm = (jnp.where(lane < HEAD_DIM, q2, zero), jnp.where(lane >= HEAD_DIM, q2, zero))
    bias = (bias_ref[2 * p], bias_ref[2 * p + 1])
    tri = tri_ref[...]
    row = lax.broadcasted_iota(jnp.int32, (tq, tq), 0)
    col = lax.broadcasted_iota(jnp.int32, (tq, tq), 1)
    below_diag = col < row

    def block(kb, diagonal):
        start = pl.multiple_of(kb * tq, tq)
        k2 = k_ref[0, pl.ds(start, tq), :]
        v2 = v_ref[0, pl.ds(start, tq), :]
        for hh in range(2):
            z = lax.dot_general(qm[hh], k2, (((1,), (1,)), ((), ())),
                                preferred_element_type=_F32) + bias[hh]
            sp = _softplus(z)
            sp_sum = jnp.where(below_diag, sp, 0.0) if diagonal else sp
            hi, lo = _split_bf16(sp_sum)
            r = (jnp.dot(hi, tri, preferred_element_type=_F32)
                 + jnp.dot(lo, tri, preferred_element_type=_F32))
            if diagonal:
                after = r[:, 0:tq]
            else:
                after = r[:, 0:tq] + carry_ref[hh]
            w = jnp.exp(z - sp + after)
            if diagonal:
                w = jnp.where(below_diag, w, 0.0)
            pv = jnp.dot(w.astype(_BF16), v2, preferred_element_type=_F32)
            if diagonal:
                acc_ref[hh] = pv
                carry_ref[hh] = r[:, tq:2 * tq]
            else:
                acc_ref[hh] = acc_ref[hh] + pv
                carry_ref[hh] = carry_ref[hh] + r[:, tq:2 * tq]

    block(qi, True)

    def body(step, c):
        block(qi - 1 - step, False)
        return c

    lax.fori_loop(0, qi, body, 0)
    o = jnp.where(lane < HEAD_DIM, acc_ref[0], acc_ref[1])
    o_ref[0] = (o * gate_ref[0]).astype(_BF16)


def _prompt_attn_call(sb_bias, q_bf, k_bf, v_bf, gate_b, tri):
    n_seq, t, _ = q_bf.shape
    tq = Q_TILE
    n_pairs = D_ATTN // LANES
    grid = (n_seq, n_pairs, t // tq)
    tile_spec = pl.BlockSpec((1, tq, LANES), lambda b, p, qi: (b, qi, p))
    seq_spec = pl.BlockSpec((1, t, LANES), lambda b, p, qi: (b, 0, p))
    return pl.pallas_call(
        functools.partial(_prompt_attn_kernel, tq=tq),
        grid=grid,
        in_specs=[
            pl.BlockSpec(memory_space=pltpu.SMEM),
            tile_spec, seq_spec, seq_spec, tile_spec,
            pl.BlockSpec((tq, 2 * tq), lambda b, p, qi: (0, 0)),
        ],
        out_specs=tile_spec,
        out_shape=jax.ShapeDtypeStruct((n_seq, t, D_ATTN), _BF16),
        scratch_shapes=[
            pltpu.VMEM((2, tq, tq), _F32),
            pltpu.VMEM((2, tq, LANES), _F32),
        ],
        compiler_params=pltpu.CompilerParams(
            dimension_semantics=("arbitrary", "arbitrary", "arbitrary"),
            vmem_limit_bytes=VMEM_LIMIT),
        name="prompt_attn",
    )(sb_bias, q_bf, k_bf, v_bf, gate_b, tri)


def _sample_attn_kernel(pt_ref, q_ref, knew_ref, vnew_ref, k_ref, v_ref, bias_ref,
                        mpage_ref, mnew_ref, o_ref, carry_ref, *, page):
    del pt_ref
    j = pl.program_id(1)
    nq = SUBLANES
    cols = page * N_HEADS
    n_blk = cols // LANES
    q2 = q_ref[0]
    nt = (((1,), (1,)), ((), ()))

    def head_mask(width):
        hp = lax.broadcasted_iota(jnp.int32, (N_HEADS, nq, width), 0)
        ln = lax.broadcasted_iota(jnp.int32, (N_HEADS, nq, width), 2)
        return (ln & (N_HEADS - 1)) == hp

    def keep_own_head(zfull, width):
        zr = zfull.reshape(N_HEADS, nq, width)
        return jnp.sum(jnp.where(head_mask(width), zr, 0.0), axis=0)

    def spread_heads(w, width):
        wr = jnp.broadcast_to(w[None], (N_HEADS, nq, width))
        return jnp.where(head_mask(width), wr, 0.0).reshape(N_HEADS * nq, width)

    @pl.when(j == 0)
    def _():
        width = nq * N_HEADS
        zfull = lax.dot_general(q2, knew_ref[0], nt, preferred_element_type=_F32)
        z = keep_own_head(zfull, width) + bias_ref[:, 0:width]
        key = lax.broadcasted_iota(jnp.int32, (nq, width), 1) >> 3
        qry = lax.broadcasted_iota(jnp.int32, (nq, width), 0)
        valid = key < qry
        sp = _softplus(z)
        hi, lo = _split_bf16(jnp.where(valid, sp, 0.0))
        r = (jnp.dot(hi, mnew_ref[...], preferred_element_type=_F32)
             + jnp.dot(lo, mnew_ref[...], preferred_element_type=_F32))
        w = jnp.where(valid, jnp.exp(z - sp + r[:, 0:width]), 0.0)
        o_ref[0] = jnp.dot(spread_heads(w, width).astype(_BF16), vnew_ref[0],
                           preferred_element_type=_F32)
        carry_ref[...] = r[:, width:width + LANES]

    k2 = k_ref[0, 0].reshape(cols, HEAD_DIM).astype(_BF16)
    v2 = v_ref[0, 0].reshape(cols, HEAD_DIM).astype(_BF16)
    zfull = lax.dot_general(q2, k2, nt, preferred_element_type=_F32)
    z = keep_own_head(zfull, cols) + bias_ref[...]
    sp = _softplus(z)
    sp_blocks = jnp.concatenate([sp[:, b * LANES:(b + 1) * LANES] for b in range(n_blk)], axis=0)
    hi, lo = _split_bf16(sp_blocks)
    r = (jnp.dot(hi, mpage_ref[...], preferred_element_type=_F32)
         + jnp.dot(lo, mpage_ref[...], preferred_element_type=_F32))
    run = carry_ref[...]
    w_blocks = [None] * n_blk
    for b in range(n_blk - 1, -1, -1):
        rows = slice(b * nq, (b + 1) * nq)
        lanes = slice(b * LANES, (b + 1) * LANES)
        after = r[rows, 0:LANES] + run
        w_blocks[b] = spread_heads(jnp.exp(z[:, lanes] - sp[:, lanes] + after), LANES)
        run = run + r[rows, LANES:2 * LANES]
    carry_ref[...] = run
    wexp = jnp.concatenate(w_blocks, axis=1).astype(_BF16)
    o_ref[0] = o_ref[0] + jnp.dot(wexp, v2, preferred_element_type=_F32)


def _sample_attn_call(layer, page_table, q2, knew2, vnew2, cache_k, cache_v, bias_lane, mpage, mnew):
    n_seq, n_pages = page_table.shape
    page = cache_k.shape[2]
    cols = page * N_HEADS
    rows = SUBLANES * N_HEADS
    seq_spec = pl.BlockSpec((1, rows, HEAD_DIM), lambda b, j, pt: (b, 0, 0))
    page_spec = pl.BlockSpec(
        (1, 1, page, N_HEADS, HEAD_DIM),
        lambda b, j, pt: (layer, pt[b, n_pages - 1 - j], 0, 0, 0))
    const2 = lambda shape: pl.BlockSpec(shape, lambda b, j, pt: (0, 0))
    grid_spec = pltpu.PrefetchScalarGridSpec(
        num_scalar_prefetch=1,
        grid=(n_seq, n_pages),
        in_specs=[seq_spec, seq_spec, seq_spec, page_spec, page_spec,
                  const2((1, cols)), const2(mpage.shape), const2(mnew.shape)],
        out_specs=seq_spec,
        scratch_shapes=[pltpu.VMEM((SUBLANES, LANES), _F32)],
    )
    return pl.pallas_call(
        functools.partial(_sample_attn_kernel, page=page),
        grid_spec=grid_spec,
        out_shape=jax.ShapeDtypeStruct((n_seq, rows, HEAD_DIM), _F32),
        compiler_params=pltpu.CompilerParams(
            dimension_semantics=("arbitrary", "arbitrary"),
            vmem_limit_bytes=VMEM_LIMIT),
        name="sample_attn",
    )(page_table, q2, knew2, vnew2, cache_k, cache_v, bias_lane, mpage, mnew)


def _outproj_kernel(*refs, nb, tm, gate_inside, final):
    if gate_inside:
        x_ref, gate_ref, ya_ref, yb_ref, gb_ref, yc_ref, w_ref = refs[:7]
        rest = refs[7:]
    else:
        x_ref, gate_ref, ya_ref, yb_ref, yc_ref, w_ref = refs[:6]
        rest = refs[6:]
    rows = nb * tm
    if gate_inside:
        y_b = (yb_ref[...] * gb_ref[...]).astype(_BF16)
    else:
        y_b = yb_ref[...]
    out = (jnp.dot(ya_ref[...].reshape(rows, D_CONV), w_ref[0:D_CONV, :], preferred_element_type=_F32)
           + jnp.dot(y_b.reshape(rows, D_ATTN), w_ref[D_CONV:D_CONV + D_ATTN, :],
                     preferred_element_type=_F32)
           + jnp.dot(yc_ref[...].reshape(rows, D_POOL), w_ref[D_CONV + D_ATTN:, :],
                     preferred_element_type=_F32))
    xn = x_ref[...] + gate_ref[...] * out.reshape(nb, tm, D_MODEL)
    if final:
        fw_ref, y_ref = rest
        ms = jnp.mean(xn * xn, axis=-1, keepdims=True)
        y_ref[...] = xn * lax.rsqrt(ms + RMS_EPS) * fw_ref[...]
    else:
        (y_ref,) = rest
        y_ref[...] = xn


def _outproj_call(x, ada3, y_a, y_b, gate_b, y_c, w_out_bf, final_w, *, nb, tm):
    n_seq, t, _ = x.shape
    grid = (n_seq // nb, t // tm)
    row_spec = lambda width: pl.BlockSpec((nb, tm, width), lambda b, i: (b, i, 0))
    gate_inside = gate_b is not None
    final = final_w is not None
    args = [x, ada3, y_a, y_b]
    in_specs = [row_spec(D_MODEL), pl.BlockSpec((nb, 1, D_MODEL), lambda b, i: (b, 0, 2)),
                row_spec(D_CONV), row_spec(D_ATTN)]
    if gate_inside:
        args.append(gate_b)
        in_specs.append(row_spec(D_ATTN))
    args += [y_c, w_out_bf]
    in_specs += [row_spec(D_POOL), pl.BlockSpec((D_MODEL, D_MODEL), lambda b, i: (0, 0))]
    if final:
        args.append(final_w.reshape(1, D_MODEL))
        in_specs.append(pl.BlockSpec((1, D_MODEL), lambda b, i: (0, 0)))
    return pl.pallas_call(
        functools.partial(_outproj_kernel, nb=nb, tm=tm, gate_inside=gate_inside, final=final),
        grid=grid,
        in_specs=in_specs,
        out_specs=row_spec(D_MODEL),
        out_shape=jax.ShapeDtypeStruct(x.shape, _F32),
        compiler_params=pltpu.CompilerParams(
            dimension_semantics=("arbitrary", "arbitrary"),
            vmem_limit_bytes=VMEM_LIMIT),
        name="outproj",
    )(*args)


def _prompt_tri(tq):
    r = jnp.arange(tq)
    tri = jnp.where(r[:, None] > r[None, :], -1.0, 0.0)
    return jnp.concatenate([tri, -jnp.ones((tq, tq))], axis=1).astype(_BF16)


def _strided_suffix(n_keys, n_out_total):
    idx = jnp.arange(n_keys * N_HEADS)
    key, head = idx >> 3, idx & (N_HEADS - 1)
    same = head[:, None] == head[None, :]
    later = key[:, None] > key[None, :]
    within = jnp.where(same & later, -1.0, 0.0)
    head_out = jnp.arange(n_out_total) & (N_HEADS - 1)
    total = jnp.where(head[:, None] == head_out[None, :], -1.0, 0.0)
    return jnp.concatenate([within, total], axis=1).astype(_BF16)


def kernel(x_prompt, x_sample, cache_k, cache_v, state_conv, state_pool, page_table, c_prompt, c_sample,
           norm_w, w_ada, b_ada, w_in, sb_bias, conv_w, pool_w, pool_scale, w_out, final_norm_w):
    depth = norm_w.shape[0]
    n_prompt, seq, _ = x_prompt.shape
    n_sample, dec_seq, _ = x_sample.shape
    n_pages = page_table.shape[1]
    page = cache_k.shape[2]
    past_len = n_pages * page
    assert dec_seq == SUBLANES and seq % ROW_TILE == 0 and seq % Q_TILE == 0

    w_in_bf = w_in.astype(_BF16)
    w_out_bf = w_out.astype(_BF16)
    eye = jnp.eye(len(POOL_WINDOWS), dtype=pool_w.dtype)
    pool_w_bd = jnp.einsum('lgcd,gh->lgchd', pool_w, eye).reshape(depth, D_POOL, D_POOL).astype(_BF16)

    ada = _ada_call(jnp.concatenate([c_prompt, c_sample], axis=0), w_ada, b_ada)
    ada_p = ada[:, :n_prompt].reshape(depth, n_prompt, 1, 3 * D_MODEL)
    ada_s = ada[:, n_prompt:].reshape(depth, n_sample, 1, 3 * D_MODEL)

    tri = _prompt_tri(Q_TILE)
    mpage = _strided_suffix(LANES // N_HEADS, LANES)
    mnew = _strided_suffix(dec_seq, LANES)
    lane_head = jnp.arange(page * N_HEADS) & (N_HEADS - 1)

    zero_conv = jnp.zeros((n_prompt, CONV_WIDTH - 1, D_CONV), _F32)
    zero_pool = jnp.zeros((n_prompt, POOL_CTX, D_POOL), _F32)

    xp, xs = x_prompt, x_sample
    outs = [[] for _ in range(8)]
    for l in range(depth):
        last = l == depth - 1
        fw = final_norm_w if last else None

        ya, yc, gate_b, qb, kb, vb, k, v, cp, pp = _inproj_call(
            xp, ada_p[l], norm_w[l], w_in_bf[l], zero_conv, zero_pool, conv_w[l], pool_w_bd[l],
            pool_scale[l], nb=1, tm=ROW_TILE, pos0=0)
        yb = _prompt_attn_call(sb_bias[l], qb, kb, vb, gate_b, tri)
        xp = _outproj_call(xp, ada_p[l], ya, yb, None, yc, w_out_bf[l], fw, nb=1, tm=ROW_TILE)
        for lst, val in zip(outs[:4], (k, v, cp, pp)):
            lst.append(val)

        ya, yc, gate_b, qb, kb, vb, k, v, cs, ps = _inproj_call(
            xs, ada_s[l], norm_w[l], w_in_bf[l], state_conv[l], state_pool[l], conv_w[l], pool_w_bd[l],
            pool_scale[l], nb=n_sample, tm=dec_seq, pos0=past_len)
        q2 = qb.reshape(n_sample, dec_seq, N_HEADS, HEAD_DIM).transpose(0, 2, 1, 3).reshape(
            n_sample, N_HEADS * dec_seq, HEAD_DIM)
        knew2 = kb.reshape(n_sample, dec_seq * N_HEADS, HEAD_DIM)
        vnew2 = vb.reshape(n_sample, dec_seq * N_HEADS, HEAD_DIM)
        bias_lane = sb_bias[l][lane_head].reshape(1, page * N_HEADS)
        o2 = _sample_attn_call(l, page_table, q2, knew2, vnew2, cache_k, cache_v, bias_lane, mpage, mnew)
        o_s = o2.reshape(n_sample, N_HEADS, dec_seq, HEAD_DIM).transpose(0, 2, 1, 3).reshape(
            n_sample, dec_seq, D_ATTN)
        xs = _outproj_call(xs, ada_s[l], ya, o_s, gate_b, yc, w_out_bf[l], fw, nb=n_sample, tm=dec_seq)
        for lst, val in zip(outs[4:], (k, v, cs, ps)):
            lst.append(val)

    kp, vp, cpo, ppo, ks, vs, cso, pso = [jnp.stack(o) for o in outs]
    heads = lambda a: a.reshape(a.shape[:-1] + (N_HEADS, HEAD_DIM))
    return (xp, xs, heads(kp), heads(vp), cpo, ppo, heads(ks), heads(vs), cso, pso)
```

```python
import functools

import jax
import jax.numpy as jnp
from jax import lax
from jax.experimental import pallas as pl
from jax.experimental.pallas import tpu as pltpu

D_MODEL = 1024
HEAD_DIM = 64
D_ATTN = 512
N_HEADS = 8
D_CONV = 256
D_POOL = 256
CONV_WIDTH = 3
POOL_WINDOWS = (2, 4, 8, 16)
POOL_GROUP = 64
POOL_CTX = 15
RMS_EPS = 1e-6
LOG2E = 1.4426950408889634
D_IN = 4 * D_CONV + 4 * D_ATTN + 2 * D_POOL
ATTN_COL0 = 4 * D_CONV
POOL_COL0 = ATTN_COL0 + 4 * D_ATTN

LANES = 128
SUBLANES = 8
CONV_HALO = SUBLANES
POOL_HALO = 16
ROW_TILE = 512
Q_TILE = 256
ATTN_HEADS_PER_STEP = 4
PAGES_PER_STEP = 8
VMEM_LIMIT = 52 * 1024 * 1024

_BF16 = jnp.bfloat16
_F32 = jnp.float32
_NT = (((1,), (1,)), ((), ()))


def _silu(x):
    return x * jax.nn.sigmoid(x)


def _softplus(z):
    return jnp.maximum(z, 0.0) + jnp.log(1.0 + jnp.exp2(jnp.abs(z) * (-LOG2E)))


def _ada_kernel(c_ref, w_ref, b_ref, o_ref):
    a = _silu(c_ref[...]).astype(_BF16)
    w = w_ref[0].astype(_BF16)
    o_ref[0] = jnp.dot(a, w, preferred_element_type=_F32) + b_ref[0]


def _ada_call(c_all, w_ada, b_ada):
    depth = w_ada.shape[0]
    n = c_all.shape[0]
    return pl.pallas_call(
        _ada_kernel,
        grid=(depth, 3),
        in_specs=[
            pl.BlockSpec((n, D_MODEL), lambda l, j: (0, 0)),
            pl.BlockSpec((1, D_MODEL, D_MODEL), lambda l, j: (l, 0, j)),
            pl.BlockSpec((1, 1, D_MODEL), lambda l, j: (l, 0, j)),
        ],
        out_specs=pl.BlockSpec((1, n, D_MODEL), lambda l, j: (l, 0, j)),
        out_shape=jax.ShapeDtypeStruct((depth, n, 3 * D_MODEL), _F32),
        compiler_params=pltpu.CompilerParams(
            dimension_semantics=("arbitrary", "arbitrary"),
            vmem_limit_bytes=VMEM_LIMIT),
        name="ada",
    )(c_all, w_ada, b_ada.reshape(depth, 1, 3 * D_MODEL))


def _inproj_kernel(x_ref, shift_ref, scale_ref, normw_ref, w_ref, convprev_ref, poolprev_ref,
                   convw_ref, poolw_ref, pscale_ref,
                   ya_ref, yc_ref, gateb_ref, qb_ref, kb_ref, vb_ref, k_ref, v_ref,
                   newconv_ref, newpool_ref, extc_ref, extu_ref, *, nb, tm, n_tiles, pos0, q_scale):
    i = pl.program_id(1)
    rows = nb * tm

    x = x_ref[...]
    ms = jnp.mean(x * x, axis=-1, keepdims=True)
    xn = x * lax.rsqrt(ms + RMS_EPS) * normw_ref[...]
    h = xn * (1.0 + scale_ref[...]) + shift_ref[...]
    h2 = h.reshape(rows, D_MODEL).astype(_BF16)

    @pl.when(i == 0)
    def _():
        extc_ref[:, CONV_HALO - 2:CONV_HALO, :] = convprev_ref[...]
        extu_ref[:, POOL_HALO - POOL_CTX:POOL_HALO, :] = poolprev_ref[...]

    pc = jnp.dot(h2, w_ref[:, 0:ATTN_COL0], preferred_element_type=_F32)
    a_b = pc[:, 0:D_CONV].reshape(nb, tm, D_CONV)
    a_c = pc[:, D_CONV:2 * D_CONV]
    a_h = pc[:, 2 * D_CONV:3 * D_CONV]
    a_z = pc[:, 3 * D_CONV:4 * D_CONV].reshape(nb, tm, D_CONV)
    conv_in = (a_c * a_h).reshape(nb, tm, D_CONV)
    extc_ref[:, CONV_HALO:CONV_HALO + tm, :] = conv_in
    cw = convw_ref[...]
    conv_out = (extc_ref[:, CONV_HALO - 2:CONV_HALO - 2 + tm, :] * cw[0:1]
                + extc_ref[:, CONV_HALO - 1:CONV_HALO - 1 + tm, :] * cw[1:2]
                + conv_in * cw[2:3])
    ya_ref[...] = (a_b * conv_out * _silu(a_z)).astype(_BF16)
    last_conv = extc_ref[:, CONV_HALO + tm - 2:CONV_HALO + tm, :]
    newconv_ref[...] = last_conv
    if n_tiles > 1:
        extc_ref[:, CONV_HALO - 2:CONV_HALO, :] = last_conv

    pa = jnp.dot(h2, w_ref[:, ATTN_COL0:POOL_COL0], preferred_element_type=_F32)
    q = pa[:, 0:D_ATTN].reshape(nb, tm, D_ATTN)
    k = pa[:, D_ATTN:2 * D_ATTN].reshape(nb, tm, D_ATTN)
    v = pa[:, 2 * D_ATTN:3 * D_ATTN].reshape(nb, tm, D_ATTN)
    b_z = pa[:, 3 * D_ATTN:4 * D_ATTN].reshape(nb, tm, D_ATTN)
    qb_ref[...] = (q * q_scale).astype(_BF16)
    k_ref[...] = k
    v_ref[...] = v
    kb_ref[...] = k.astype(_BF16)
    vb_ref[...] = v.astype(_BF16)
    gateb_ref[...] = _silu(b_z)

    pp = jnp.dot(h2, w_ref[:, POOL_COL0:D_IN], preferred_element_type=_F32)
    p_u = pp[:, 0:D_POOL].reshape(nb, tm, D_POOL)
    p_z = pp[:, D_POOL:2 * D_POOL].reshape(nb, tm, D_POOL)
    extu_ref[:, POOL_HALO:POOL_HALO + tm, :] = p_u

    def win_sum(lo, hi, lane0):
        acc = extu_ref[:, POOL_HALO - lo:POOL_HALO - lo + tm, lane0:lane0 + LANES]
        for kk in range(lo + 1, hi):
            acc = acc + extu_ref[:, POOL_HALO - kk:POOL_HALO - kk + tm, lane0:lane0 + LANES]
        return acc

    s2 = win_sum(0, 2, 0)
    s4 = s2 + win_sum(2, 4, 0)
    s8 = win_sum(0, 8, LANES)
    s16 = s8 + win_sum(8, 16, LANES)
    pos = pos0 + i * tm + lax.broadcasted_iota(jnp.int32, (nb, tm, LANES), 1)
    lane = lax.broadcasted_iota(jnp.int32, (nb, tm, LANES), 2)
    first = lane < POOL_GROUP

    def cnt(win):
        return jnp.minimum(win, pos + 1).astype(_F32)

    mean_lo = jnp.where(first, s2 / cnt(2), s4 / cnt(4))
    mean_hi = jnp.where(first, s8 / cnt(8), s16 / cnt(16))
    pooled = jnp.concatenate([mean_lo, mean_hi], axis=-1) - p_u
    y_c = jnp.dot(pooled.reshape(rows, D_POOL).astype(_BF16), poolw_ref[...],
                  preferred_element_type=_F32).reshape(nb, tm, D_POOL)
    yc_ref[...] = (y_c * pscale_ref[...] * _silu(p_z)).astype(_BF16)
    last_pool = extu_ref[:, POOL_HALO + tm - POOL_CTX:POOL_HALO + tm, :]
    newpool_ref[...] = last_pool
    if n_tiles > 1:
        extu_ref[:, POOL_HALO - POOL_CTX:POOL_HALO, :] = last_pool


def _inproj_call(x, ada3, norm_w, w_in_bf, conv_prev, pool_prev, conv_w, pool_w_bd, pool_scale,
                 *, nb, tm, pos0, q_scale):
    n_seq, t, _ = x.shape
    n_tiles = t // tm
    grid = (n_seq // nb, n_tiles)
    row_spec = lambda width: pl.BlockSpec((nb, tm, width), lambda b, i: (b, i, 0))
    ada_spec = lambda j: pl.BlockSpec((nb, 1, D_MODEL), lambda b, i, j=j: (b, 0, j))
    const2 = lambda shape: pl.BlockSpec(shape, lambda b, i: (0, 0))
    state_spec = lambda r, c: pl.BlockSpec((nb, r, c), lambda b, i: (b, 0, 0))
    sds = jax.ShapeDtypeStruct
    out_shape = (
        sds((n_seq, t, D_CONV), _BF16), sds((n_seq, t, D_POOL), _BF16),
        sds((n_seq, t, D_ATTN), _F32),
        sds((n_seq, t, D_ATTN), _BF16), sds((n_seq, t, D_ATTN), _BF16), sds((n_seq, t, D_ATTN), _BF16),
        sds((n_seq, t, D_ATTN), _F32), sds((n_seq, t, D_ATTN), _F32),
        sds((n_seq, CONV_WIDTH - 1, D_CONV), _F32), sds((n_seq, POOL_CTX, D_POOL), _F32),
    )
    out_specs = (
        row_spec(D_CONV), row_spec(D_POOL), row_spec(D_ATTN),
        row_spec(D_ATTN), row_spec(D_ATTN), row_spec(D_ATTN),
        row_spec(D_ATTN), row_spec(D_ATTN),
        state_spec(CONV_WIDTH - 1, D_CONV), state_spec(POOL_CTX, D_POOL),
    )
    return pl.pallas_call(
        functools.partial(_inproj_kernel, nb=nb, tm=tm, n_tiles=n_tiles, pos0=pos0, q_scale=q_scale),
        grid=grid,
        in_specs=[
            row_spec(D_MODEL), ada_spec(0), ada_spec(1),
            const2((1, D_MODEL)), const2((D_MODEL, D_IN)),
            state_spec(CONV_WIDTH - 1, D_CONV), state_spec(POOL_CTX, D_POOL),
            const2((CONV_WIDTH, D_CONV)), const2((D_POOL, D_POOL)), const2((1, D_POOL)),
        ],
        out_specs=out_specs,
        out_shape=out_shape,
        scratch_shapes=[
            pltpu.VMEM((nb, CONV_HALO + tm, D_CONV), _F32),
            pltpu.VMEM((nb, POOL_HALO + tm, D_POOL), _F32),
        ],
        compiler_params=pltpu.CompilerParams(
            dimension_semantics=("arbitrary", "arbitrary"),
            vmem_limit_bytes=VMEM_LIMIT),
        name="inproj",
    )(x, ada3, ada3, norm_w.reshape(1, D_MODEL), w_in_bf, conv_prev, pool_prev,
      conv_w, pool_w_bd, pool_scale.reshape(1, D_POOL))


def _prompt_attn_kernel(bias_ref, q_ref, k_ref, v_ref, gate_ref, tri_ref, o_ref,
                        carry_ref, acc_ref, *, tq, n_heads):
    g = pl.program_id(1)
    qi = pl.program_id(2)
    n_pairs = n_heads // 2
    lane = lax.broadcasted_iota(jnp.int32, (tq, LANES), 1)
    low = lane < HEAD_DIM
    qm, half_bias = [], []
    for hp in range(n_pairs):
        q2 = q_ref[0, :, hp * LANES:(hp + 1) * LANES]
        zero = jnp.zeros_like(q2)
        qm += [jnp.where(low, q2, zero), jnp.where(low, zero, q2)]
        half_bias += [0.5 * bias_ref[g * n_heads + 2 * hp], 0.5 * bias_ref[g * n_heads + 2 * hp + 1]]
    tri = tri_ref[...]
    row = lax.broadcasted_iota(jnp.int32, (tq, tq), 0)
    col = lax.broadcasted_iota(jnp.int32, (tq, tq), 1)
    below_diag = col < row

    def block(kb, diagonal):
        start = pl.multiple_of(kb * tq, tq)
        pair_lanes = lambda h: slice((h // 2) * LANES, (h // 2 + 1) * LANES)
        sps, totals, logsigs, ws = [], [], [], []
        for h in range(n_heads):
            k2 = k_ref[0, pl.ds(start, tq), pair_lanes(h)]
            zh = lax.dot_general(qm[h], k2, _NT, preferred_element_type=_F32) + half_bias[h]
            ah = jnp.abs(zh)
            l1p = jnp.log(1.0 + jnp.exp2(ah * (-2.0 * LOG2E)))
            sp = (zh + ah) + l1p
            if diagonal:
                sp = jnp.where(below_diag, sp, 0.0)
            sps.append(sp.astype(_BF16))
            totals.append(-jnp.sum(sp, axis=1, keepdims=True))
            logsigs.append((zh - ah) - l1p)
        for h in range(n_heads):
            within = jnp.dot(sps[h], tri, preferred_element_type=_F32)
            w = jnp.exp(logsigs[h] + within)
            if diagonal:
                w = jnp.where(below_diag, w, 0.0)
            ws.append(w.astype(_BF16))
        for h in range(n_heads):
            v2 = v_ref[0, pl.ds(start, tq), pair_lanes(h)]
            pv = jnp.dot(ws[h], v2, preferred_element_type=_F32)
            if diagonal:
                acc_ref[h] = pv
                carry_ref[h] = totals[h]
            else:
                carry = carry_ref[h]
                acc_ref[h] = acc_ref[h] + pv * jnp.exp(carry)
                carry_ref[h] = carry + totals[h]

    block(qi, True)

    def body(step, c):
        block(qi - 1 - step, False)
        return c

    lax.fori_loop(0, qi, body, 0)
    for hp in range(n_pairs):
        o = jnp.where(low, acc_ref[2 * hp], acc_ref[2 * hp + 1])
        lanes = slice(hp * LANES, (hp + 1) * LANES)
        o_ref[0, :, lanes] = (o * gate_ref[0, :, lanes]).astype(_BF16)


def _prompt_attn_call(sb_bias, q_bf, k_bf, v_bf, gate_b, tri):
    n_seq, t, _ = q_bf.shape
    tq = Q_TILE
    n_heads = ATTN_HEADS_PER_STEP
    width = n_heads * HEAD_DIM
    grid = (n_seq, D_ATTN // width, t // tq)
    tile_spec = pl.BlockSpec((1, tq, width), lambda b, g, qi: (b, qi, g))
    seq_spec = pl.BlockSpec((1, t, width), lambda b, g, qi: (b, 0, g))
    return pl.pallas_call(
        functools.partial(_prompt_attn_kernel, tq=tq, n_heads=n_heads),
        grid=grid,
        in_specs=[
            pl.BlockSpec(memory_space=pltpu.SMEM),
            tile_spec, seq_spec, seq_spec, tile_spec,
            pl.BlockSpec((tq, tq), lambda b, g, qi: (0, 0)),
        ],
        out_specs=tile_spec,
        out_shape=jax.ShapeDtypeStruct((n_seq, t, D_ATTN), _BF16),
        scratch_shapes=[
            pltpu.VMEM((n_heads, tq, 1), _F32),
            pltpu.VMEM((n_heads, tq, LANES), _F32),
        ],
        compiler_params=pltpu.CompilerParams(
            dimension_semantics=("arbitrary", "arbitrary", "arbitrary"),
            vmem_limit_bytes=VMEM_LIMIT),
        name="prompt_attn",
    )(sb_bias, q_bf, k_bf, v_bf, gate_b, tri)


def _sample_attn_kernel(pt_ref, qbd_ref, bias_ref, knew_ref, vnew_ref, *rest, n_g, page):
    del pt_ref
    k_refs = rest[:n_g]
    v_refs = rest[n_g:2 * n_g]
    tri_ref, o_ref, acc_ref, carry_ref = rest[2 * n_g:]
    j = pl.program_id(1)
    nq = SUBLANES
    rows = N_HEADS * nq
    qbd = qbd_ref[0]
    bias = bias_ref[...]
    tri = tri_ref[...]

    @pl.when(j == 0)
    def _():
        pad = jnp.zeros((page - nq, D_ATTN), _F32)
        knew = jnp.concatenate([knew_ref[0], pad], axis=0).astype(_BF16)
        vnew = jnp.concatenate([vnew_ref[0], pad], axis=0).astype(_BF16)
        z = lax.dot_general(qbd, knew, _NT, preferred_element_type=_F32) + bias
        key = lax.broadcasted_iota(jnp.int32, (rows, page), 1)
        qry = lax.broadcasted_iota(jnp.int32, (rows, page), 0) & (nq - 1)
        valid = key < qry
        sp = _softplus(z)
        r = jnp.dot(jnp.where(valid, sp, 0.0).astype(_BF16), tri, preferred_element_type=_F32)
        w = jnp.where(valid, jnp.exp(z - sp + r[:, 0:page]), 0.0)
        acc_ref[...] = jnp.dot(w.astype(_BF16), vnew, preferred_element_type=_F32)
        carry_ref[...] = r[:, page:2 * page]

    zs, sps = [], []
    for g in range(n_g):
        kt = k_refs[g][0, 0].reshape(D_ATTN, page).astype(_BF16)
        z = jnp.dot(qbd, kt, preferred_element_type=_F32) + bias
        zs.append(z)
        sps.append(_softplus(z))
    r = jnp.dot(jnp.concatenate(sps, axis=0).astype(_BF16), tri, preferred_element_type=_F32)
    run = carry_ref[...]
    pv = None
    for g in range(n_g):
        blk = slice(g * rows, (g + 1) * rows)
        w = jnp.exp(zs[g] - sps[g] + (r[blk, 0:page] + run))
        run = run + r[blk, page:2 * page]
        vt = v_refs[g][0, 0].reshape(D_ATTN, page).astype(_BF16)
        c = lax.dot_general(w.astype(_BF16), vt, _NT, preferred_element_type=_F32)
        pv = c if pv is None else pv + c
    carry_ref[...] = run
    acc_ref[...] = acc_ref[...] + pv

    @pl.when(j == pl.num_programs(1) - 1)
    def _():
        acc = acc_ref[...]
        o_ref[0] = jnp.concatenate(
            [acc[h * nq:(h + 1) * nq, h * HEAD_DIM:(h + 1) * HEAD_DIM] for h in range(N_HEADS)], axis=0)


def _sample_attn_call(layer, page_table, qbd, bias_rows, k_new, v_new, cache_kt, cache_vt, tri):
    n_seq, n_pages = page_table.shape
    page = cache_kt.shape[-1]
    n_g = PAGES_PER_STEP
    rows = SUBLANES * N_HEADS
    seq_spec = lambda r, c: pl.BlockSpec((1, r, c), lambda b, j, pt: (b, 0, 0))
    const2 = lambda shape: pl.BlockSpec(shape, lambda b, j, pt: (0, 0))

    def page_spec(g):
        return pl.BlockSpec(
            (1, 1, N_HEADS, HEAD_DIM, page),
            lambda b, j, pt, g=g: (layer, pt[b, n_pages - 1 - (j * n_g + g)], 0, 0, 0))

    page_specs = [page_spec(g) for g in range(n_g)]
    grid_spec = pltpu.PrefetchScalarGridSpec(
        num_scalar_prefetch=1,
        grid=(n_seq, n_pages // n_g),
        in_specs=[seq_spec(rows, D_ATTN), const2((rows, page)),
                  seq_spec(SUBLANES, D_ATTN), seq_spec(SUBLANES, D_ATTN)]
                 + page_specs + page_specs + [const2(tri.shape)],
        out_specs=seq_spec(rows, HEAD_DIM),
        scratch_shapes=[pltpu.VMEM((rows, D_ATTN), _F32), pltpu.VMEM((rows, page), _F32)],
    )
    return pl.pallas_call(
        functools.partial(_sample_attn_kernel, n_g=n_g, page=page),
        grid_spec=grid_spec,
        out_shape=jax.ShapeDtypeStruct((n_seq, rows, HEAD_DIM), _F32),
        compiler_params=pltpu.CompilerParams(
            dimension_semantics=("arbitrary", "arbitrary"),
            vmem_limit_bytes=VMEM_LIMIT),
        name="sample_attn",
    )(page_table, qbd, bias_rows, k_new, v_new, *([cache_kt] * n_g), *([cache_vt] * n_g), tri)


def _outproj_kernel(*refs, nb, tm, gate_inside, final):
    if gate_inside:
        x_ref, gate_ref, ya_ref, yb_ref, gb_ref, yc_ref, w_ref = refs[:7]
        rest = refs[7:]
    else:
        x_ref, gate_ref, ya_ref, yb_ref, yc_ref, w_ref = refs[:6]
        rest = refs[6:]
    rows = nb * tm
    if gate_inside:
        y_b = (yb_ref[...] * gb_ref[...]).astype(_BF16)
    else:
        y_b = yb_ref[...]
    out = (jnp.dot(ya_ref[...].reshape(rows, D_CONV), w_ref[0:D_CONV, :], preferred_element_type=_F32)
           + jnp.dot(y_b.reshape(rows, D_ATTN), w_ref[D_CONV:D_CONV + D_ATTN, :],
                     preferred_element_type=_F32)
           + jnp.dot(yc_ref[...].reshape(rows, D_POOL), w_ref[D_CONV + D_ATTN:, :],
                     preferred_element_type=_F32))
    xn = x_ref[...] + gate_ref[...] * out.reshape(nb, tm, D_MODEL)
    if final:
        fw_ref, y_ref = rest
        ms = jnp.mean(xn * xn, axis=-1, keepdims=True)
        y_ref[...] = xn * lax.rsqrt(ms + RMS_EPS) * fw_ref[...]
    else:
        (y_ref,) = rest
        y_ref[...] = xn


def _outproj_call(x, ada3, y_a, y_b, gate_b, y_c, w_out_bf, final_w, *, nb, tm):
    n_seq, t, _ = x.shape
    grid = (n_seq // nb, t // tm)
    row_spec = lambda width: pl.BlockSpec((nb, tm, width), lambda b, i: (b, i, 0))
    gate_inside = gate_b is not None
    final = final_w is not None
    args = [x, ada3, y_a, y_b]
    in_specs = [row_spec(D_MODEL), pl.BlockSpec((nb, 1, D_MODEL), lambda b, i: (b, 0, 2)),
                row_spec(D_CONV), row_spec(D_ATTN)]
    if gate_inside:
        args.append(gate_b)
        in_specs.append(row_spec(D_ATTN))
    args += [y_c, w_out_bf]
    in_specs += [row_spec(D_POOL), pl.BlockSpec((D_MODEL, D_MODEL), lambda b, i: (0, 0))]
    if final:
        args.append(final_w.reshape(1, D_MODEL))
        in_specs.append(pl.BlockSpec((1, D_MODEL), lambda b, i: (0, 0)))
    return pl.pallas_call(
        functools.partial(_outproj_kernel, nb=nb, tm=tm, gate_inside=gate_inside, final=final),
        grid=grid,
        in_specs=in_specs,
        out_specs=row_spec(D_MODEL),
        out_shape=jax.ShapeDtypeStruct(x.shape, _F32),
        compiler_params=pltpu.CompilerParams(
            dimension_semantics=("arbitrary", "arbitrary"),
            vmem_limit_bytes=VMEM_LIMIT),
        name="outproj",
    )(*args)


def _suffix_matrix(n):
    r = jnp.arange(n)
    tri = jnp.where(r[:, None] > r[None, :], -1.0, 0.0)
    return jnp.concatenate([tri, -jnp.ones((n, n))], axis=1).astype(_BF16)


def kernel(x_prompt, x_sample, cache_k, cache_v, state_conv, state_pool, page_table, c_prompt, c_sample,
           norm_w, w_ada, b_ada, w_in, sb_bias, conv_w, pool_w, pool_scale, w_out, final_norm_w):
    depth = norm_w.shape[0]
    n_prompt, seq, _ = x_prompt.shape
    n_sample, dec_seq, _ = x_sample.shape
    n_pages = page_table.shape[1]
    page = cache_k.shape[2]
    past_len = n_pages * page
    assert dec_seq == SUBLANES and seq % ROW_TILE == 0 and seq % Q_TILE == 0
    assert n_pages % PAGES_PER_STEP == 0 and page == LANES

    w_in_bf = w_in.astype(_BF16)
    w_out_bf = w_out.astype(_BF16)
    eye = jnp.eye(N_HEADS, dtype=_F32)
    pool_w_bd = jnp.einsum('lgcd,gh->lgchd', pool_w, eye[:len(POOL_WINDOWS), :len(POOL_WINDOWS)]).reshape(
        depth, D_POOL, D_POOL).astype(_BF16)
    cache_kt = cache_k.transpose(0, 1, 3, 4, 2)
    cache_vt = cache_v.transpose(0, 1, 3, 4, 2)

    ada = _ada_call(jnp.concatenate([c_prompt, c_sample], axis=0), w_ada, b_ada)
    ada_p = ada[:, :n_prompt].reshape(depth, n_prompt, 1, 3 * D_MODEL)
    ada_s = ada[:, n_prompt:].reshape(depth, n_sample, 1, 3 * D_MODEL)

    tri_prompt = _suffix_matrix(Q_TILE)[:, :Q_TILE]
    tri_page = _suffix_matrix(page)
    head_of_row = jnp.arange(N_HEADS * dec_seq) // dec_seq

    zero_conv = jnp.zeros((n_prompt, CONV_WIDTH - 1, D_CONV), _F32)
    zero_pool = jnp.zeros((n_prompt, POOL_CTX, D_POOL), _F32)

    xp, xs = x_prompt, x_sample
    outs = [[] for _ in range(8)]
    for l in range(depth):
        last = l == depth - 1
        fw = final_norm_w if last else None

        ya, yc, gate_b, qb, kb, vb, k, v, cp, pp = _inproj_call(
            xp, ada_p[l], norm_w[l], w_in_bf[l], zero_conv, zero_pool, conv_w[l], pool_w_bd[l],
            pool_scale[l], nb=1, tm=ROW_TILE, pos0=0, q_scale=0.5 * HEAD_DIM ** -0.5)
        yb = _prompt_attn_call(sb_bias[l], qb, kb, vb, gate_b, tri_prompt)
        xp = _outproj_call(xp, ada_p[l], ya, yb, None, yc, w_out_bf[l], fw, nb=1, tm=ROW_TILE)
        for lst, val in zip(outs[:4], (k, v, cp, pp)):
            lst.append(val)

        ya, yc, gate_b, qb, _, _, k, v, cs, ps = _inproj_call(
            xs, ada_s[l], norm_w[l], w_in_bf[l], state_conv[l], state_pool[l], conv_w[l], pool_w_bd[l],
            pool_scale[l], nb=n_sample, tm=dec_seq, pos0=past_len, q_scale=HEAD_DIM ** -0.5)
        q_hq = qb.reshape(n_sample, dec_seq, N_HEADS, HEAD_DIM).transpose(0, 2, 1, 3)
        qbd = (q_hq[:, :, :, None, :] * eye.astype(_BF16)[None, :, None, :, None]).reshape(
            n_sample, N_HEADS * dec_seq, D_ATTN)
        bias_rows = jnp.broadcast_to(sb_bias[l][head_of_row][:, None], (N_HEADS * dec_seq, page))
        o2 = _sample_attn_call(l, page_table, qbd, bias_rows, k, v, cache_kt, cache_vt, tri_page)
        o_s = o2.reshape(n_sample, N_HEADS, dec_seq, HEAD_DIM).transpose(0, 2, 1, 3).reshape(
            n_sample, dec_seq, D_ATTN)
        xs = _outproj_call(xs, ada_s[l], ya, o_s, gate_b, yc, w_out_bf[l], fw, nb=n_sample, tm=dec_seq)
        for lst, val in zip(outs[4:], (k, v, cs, ps)):
            lst.append(val)

    kp, vp, cpo, ppo, ks, vs, cso, pso = [jnp.stack(o) for o in outs]
    heads = lambda a: a.reshape(a.shape[:-1] + (N_HEADS, HEAD_DIM))
    return (xp, xs, heads(kp), heads(vp), cpo, ppo, heads(ks), heads(vs), cso, pso)
```

```python
import functools

import jax
import jax.numpy as jnp
from jax import lax
from jax.experimental import pallas as pl
from jax.experimental.pallas import tpu as pltpu

D_MODEL = 1024
HEAD_DIM = 64
D_ATTN = 512
N_HEADS = 8
D_CONV = 256
D_POOL = 256
CONV_WIDTH = 3
POOL_WINDOWS = (2, 4, 8, 16)
POOL_GROUP = 64
POOL_CTX = 15
RMS_EPS = 1e-6
LOG2E = 1.4426950408889634
D_IN = 4 * D_CONV + 4 * D_ATTN + 2 * D_POOL
ATTN_COL0 = 4 * D_CONV
POOL_COL0 = ATTN_COL0 + 4 * D_ATTN

LANES = 128
SUBLANES = 8
CONV_HALO = SUBLANES
POOL_HALO = 16
ROW_TILE = 512
Q_TILE = 256
ATTN_HEADS_PER_STEP = 4
PAGES_PER_STEP = 16
VMEM_LIMIT = 52 * 1024 * 1024

_BF16 = jnp.bfloat16
_F32 = jnp.float32
_NT = (((1,), (1,)), ((), ()))


def _silu(x):
    return x * jax.nn.sigmoid(x)


def _softplus(z):
    return jnp.maximum(z, 0.0) + jnp.log(1.0 + jnp.exp2(jnp.abs(z) * (-LOG2E)))


def _ada_kernel(c_ref, w_ref, b_ref, o_ref):
    a = _silu(c_ref[...]).astype(_BF16)
    w = w_ref[0].astype(_BF16)
    o_ref[0] = jnp.dot(a, w, preferred_element_type=_F32) + b_ref[0]


def _ada_call(c_all, w_ada, b_ada):
    depth = w_ada.shape[0]
    n = c_all.shape[0]
    return pl.pallas_call(
        _ada_kernel,
        grid=(depth, 3),
        in_specs=[
            pl.BlockSpec((n, D_MODEL), lambda l, j: (0, 0)),
            pl.BlockSpec((1, D_MODEL, D_MODEL), lambda l, j: (l, 0, j)),
            pl.BlockSpec((1, 1, D_MODEL), lambda l, j: (l, 0, j)),
        ],
        out_specs=pl.BlockSpec((1, n, D_MODEL), lambda l, j: (l, 0, j)),
        out_shape=jax.ShapeDtypeStruct((depth, n, 3 * D_MODEL), _F32),
        compiler_params=pltpu.CompilerParams(
            dimension_semantics=("arbitrary", "arbitrary"),
            vmem_limit_bytes=VMEM_LIMIT),
        name="ada",
    )(c_all, w_ada, b_ada.reshape(depth, 1, 3 * D_MODEL))


def _inproj_kernel(x_ref, shift_ref, scale_ref, normw_ref, w_ref, convprev_ref, poolprev_ref,
                   convw_ref, poolw_ref, pscale_ref,
                   ya_ref, yc_ref, gateb_ref, qb_ref, kb_ref, vb_ref, k_ref, v_ref,
                   newconv_ref, newpool_ref, extc_ref, extu_ref, *, nb, tm, n_tiles, pos0, q_scale):
    i = pl.program_id(1)
    rows = nb * tm

    x = x_ref[...]
    ms = jnp.mean(x * x, axis=-1, keepdims=True)
    xn = x * lax.rsqrt(ms + RMS_EPS) * normw_ref[...]
    h = xn * (1.0 + scale_ref[...]) + shift_ref[...]
    h2 = h.reshape(rows, D_MODEL).astype(_BF16)

    @pl.when(i == 0)
    def _():
        extc_ref[:, CONV_HALO - 2:CONV_HALO, :] = convprev_ref[...]
        extu_ref[:, POOL_HALO - POOL_CTX:POOL_HALO, :] = poolprev_ref[...]

    pc = jnp.dot(h2, w_ref[:, 0:ATTN_COL0], preferred_element_type=_F32)
    a_b = pc[:, 0:D_CONV].reshape(nb, tm, D_CONV)
    a_c = pc[:, D_CONV:2 * D_CONV]
    a_h = pc[:, 2 * D_CONV:3 * D_CONV]
    a_z = pc[:, 3 * D_CONV:4 * D_CONV].reshape(nb, tm, D_CONV)
    conv_in = (a_c * a_h).reshape(nb, tm, D_CONV)
    extc_ref[:, CONV_HALO:CONV_HALO + tm, :] = conv_in
    cw = convw_ref[...]
    conv_out = (extc_ref[:, CONV_HALO - 2:CONV_HALO - 2 + tm, :] * cw[0:1]
                + extc_ref[:, CONV_HALO - 1:CONV_HALO - 1 + tm, :] * cw[1:2]
                + conv_in * cw[2:3])
    ya_ref[...] = (a_b * conv_out * _silu(a_z)).astype(_BF16)
    last_conv = extc_ref[:, CONV_HALO + tm - 2:CONV_HALO + tm, :]
    newconv_ref[...] = last_conv
    if n_tiles > 1:
        extc_ref[:, CONV_HALO - 2:CONV_HALO, :] = last_conv

    pa = jnp.dot(h2, w_ref[:, ATTN_COL0:POOL_COL0], preferred_element_type=_F32)
    q = pa[:, 0:D_ATTN].reshape(nb, tm, D_ATTN)
    k = pa[:, D_ATTN:2 * D_ATTN].reshape(nb, tm, D_ATTN)
    v = pa[:, 2 * D_ATTN:3 * D_ATTN].reshape(nb, tm, D_ATTN)
    b_z = pa[:, 3 * D_ATTN:4 * D_ATTN].reshape(nb, tm, D_ATTN)
    qb_ref[...] = (q * q_scale).astype(_BF16)
    k_ref[...] = k
    v_ref[...] = v
    kb_ref[...] = k.astype(_BF16)
    vb_ref[...] = v.astype(_BF16)
    gateb_ref[...] = _silu(b_z)

    pp = jnp.dot(h2, w_ref[:, POOL_COL0:D_IN], preferred_element_type=_F32)
    p_u = pp[:, 0:D_POOL].reshape(nb, tm, D_POOL)
    p_z = pp[:, D_POOL:2 * D_POOL].reshape(nb, tm, D_POOL)
    extu_ref[:, POOL_HALO:POOL_HALO + tm, :] = p_u

    def win_sum(lo, hi, lane0):
        acc = extu_ref[:, POOL_HALO - lo:POOL_HALO - lo + tm, lane0:lane0 + LANES]
        for kk in range(lo + 1, hi):
            acc = acc + extu_ref[:, POOL_HALO - kk:POOL_HALO - kk + tm, lane0:lane0 + LANES]
        return acc

    s2 = win_sum(0, 2, 0)
    s4 = s2 + win_sum(2, 4, 0)
    s8 = win_sum(0, 8, LANES)
    s16 = s8 + win_sum(8, 16, LANES)
    pos = pos0 + i * tm + lax.broadcasted_iota(jnp.int32, (nb, tm, LANES), 1)
    lane = lax.broadcasted_iota(jnp.int32, (nb, tm, LANES), 2)
    first = lane < POOL_GROUP

    def cnt(win):
        return jnp.minimum(win, pos + 1).astype(_F32)

    mean_lo = jnp.where(first, s2 / cnt(2), s4 / cnt(4))
    mean_hi = jnp.where(first, s8 / cnt(8), s16 / cnt(16))
    pooled = jnp.concatenate([mean_lo, mean_hi], axis=-1) - p_u
    y_c = jnp.dot(pooled.reshape(rows, D_POOL).astype(_BF16), poolw_ref[...],
                  preferred_element_type=_F32).reshape(nb, tm, D_POOL)
    yc_ref[...] = (y_c * pscale_ref[...] * _silu(p_z)).astype(_BF16)
    last_pool = extu_ref[:, POOL_HALO + tm - POOL_CTX:POOL_HALO + tm, :]
    newpool_ref[...] = last_pool
    if n_tiles > 1:
        extu_ref[:, POOL_HALO - POOL_CTX:POOL_HALO, :] = last_pool


def _inproj_call(x, ada3, norm_w, w_in_bf, conv_prev, pool_prev, conv_w, pool_w_bd, pool_scale,
                 *, nb, tm, pos0, q_scale):
    n_seq, t, _ = x.shape
    n_tiles = t // tm
    grid = (n_seq // nb, n_tiles)
    row_spec = lambda width: pl.BlockSpec((nb, tm, width), lambda b, i: (b, i, 0))
    ada_spec = lambda j: pl.BlockSpec((nb, 1, D_MODEL), lambda b, i, j=j: (b, 0, j))
    const2 = lambda shape: pl.BlockSpec(shape, lambda b, i: (0, 0))
    state_spec = lambda r, c: pl.BlockSpec((nb, r, c), lambda b, i: (b, 0, 0))
    sds = jax.ShapeDtypeStruct
    out_shape = (
        sds((n_seq, t, D_CONV), _BF16), sds((n_seq, t, D_POOL), _BF16),
        sds((n_seq, t, D_ATTN), _F32),
        sds((n_seq, t, D_ATTN), _BF16), sds((n_seq, t, D_ATTN), _BF16), sds((n_seq, t, D_ATTN), _BF16),
        sds((n_seq, t, D_ATTN), _F32), sds((n_seq, t, D_ATTN), _F32),
        sds((n_seq, CONV_WIDTH - 1, D_CONV), _F32), sds((n_seq, POOL_CTX, D_POOL), _F32),
    )
    out_specs = (
        row_spec(D_CONV), row_spec(D_POOL), row_spec(D_ATTN),
        row_spec(D_ATTN), row_spec(D_ATTN), row_spec(D_ATTN),
        row_spec(D_ATTN), row_spec(D_ATTN),
        state_spec(CONV_WIDTH - 1, D_CONV), state_spec(POOL_CTX, D_POOL),
    )
    return pl.pallas_call(
        functools.partial(_inproj_kernel, nb=nb, tm=tm, n_tiles=n_tiles, pos0=pos0, q_scale=q_scale),
        grid=grid,
        in_specs=[
            row_spec(D_MODEL), ada_spec(0), ada_spec(1),
            const2((1, D_MODEL)), const2((D_MODEL, D_IN)),
            state_spec(CONV_WIDTH - 1, D_CONV), state_spec(POOL_CTX, D_POOL),
            const2((CONV_WIDTH, D_CONV)), const2((D_POOL, D_POOL)), const2((1, D_POOL)),
        ],
        out_specs=out_specs,
        out_shape=out_shape,
        scratch_shapes=[
            pltpu.VMEM((nb, CONV_HALO + tm, D_CONV), _F32),
            pltpu.VMEM((nb, POOL_HALO + tm, D_POOL), _F32),
        ],
        compiler_params=pltpu.CompilerParams(
            dimension_semantics=("arbitrary", "arbitrary"),
            vmem_limit_bytes=VMEM_LIMIT),
        name="inproj",
    )(x, ada3, ada3, norm_w.reshape(1, D_MODEL), w_in_bf, conv_prev, pool_prev,
      conv_w, pool_w_bd, pool_scale.reshape(1, D_POOL))


def _prompt_attn_kernel(bias_ref, q_ref, k_ref, v_ref, gate_ref, tri_ref, o_ref,
                        carry_ref, acc_ref, zha_ref, zhb_ref, *, tq, n_heads):
    g = pl.program_id(1)
    qi = pl.program_id(2)
    n_pairs = n_heads // 2
    lane = lax.broadcasted_iota(jnp.int32, (tq, LANES), 1)
    low = lane < HEAD_DIM
    qm, half_bias = [], []
    for hp in range(n_pairs):
        q2 = q_ref[0, :, hp * LANES:(hp + 1) * LANES]
        zero = jnp.zeros_like(q2)
        qm += [jnp.where(low, q2, zero), jnp.where(low, zero, q2)]
        half_bias += [0.5 * bias_ref[g * n_heads + 2 * hp], 0.5 * bias_ref[g * n_heads + 2 * hp + 1]]
    tri = tri_ref[...]
    row = lax.broadcasted_iota(jnp.int32, (tq, tq), 0)
    col = lax.broadcasted_iota(jnp.int32, (tq, tq), 1)
    below_diag = col < row

    pair_lanes = lambda h: slice((h // 2) * LANES, (h // 2 + 1) * LANES)

    def half_scores(kb):
        start = pl.multiple_of(kb * tq, tq)
        out = []
        for h in range(n_heads):
            k2 = k_ref[0, pl.ds(start, tq), pair_lanes(h)]
            out.append(lax.dot_general(qm[h], k2, _NT, preferred_element_type=_F32) + half_bias[h])
        return tuple(out)

    def block(kb, zhs, diagonal):
        start = pl.multiple_of(kb * tq, tq)
        sps, totals, logsigs, ws = [], [], [], []
        for h in range(n_heads):
            zh = zhs[h]
            ah = jnp.abs(zh)
            l1p = jnp.log(1.0 + jnp.exp2(ah * (-2.0 * LOG2E)))
            sp = (zh + ah) + l1p
            if diagonal:
                sp = jnp.where(below_diag, sp, 0.0)
            sps.append(sp.astype(_BF16))
            totals.append(-jnp.sum(sp, axis=1, keepdims=True))
            logsigs.append((zh - ah) - l1p)
        for h in range(n_heads):
            within = jnp.dot(sps[h], tri, preferred_element_type=_F32)
            w = jnp.exp(logsigs[h] + within)
            if diagonal:
                w = jnp.where(below_diag, w, 0.0)
            ws.append(w.astype(_BF16))
        for h in range(n_heads):
            v2 = v_ref[0, pl.ds(start, tq), pair_lanes(h)]
            pv = jnp.dot(ws[h], v2, preferred_element_type=_F32)
            if diagonal:
                acc_ref[h] = pv
                carry_ref[h] = totals[h]
            else:
                carry = carry_ref[h]
                acc_ref[h] = acc_ref[h] + pv * jnp.exp(carry)
                carry_ref[h] = carry + totals[h]

    def stash(ref, kb):
        for h, zh in enumerate(half_scores(jnp.maximum(kb, 0))):
            ref[h] = zh

    def fetch(ref):
        return tuple(ref[h] for h in range(n_heads))

    stash(zha_ref, qi - 1)
    block(qi, half_scores(qi), True)

    def body(pair, c):
        kb = qi - 1 - 2 * pair
        stash(zhb_ref, kb - 1)
        block(kb, fetch(zha_ref), False)

        @pl.when(kb >= 1)
        def _():
            stash(zha_ref, kb - 2)
            block(kb - 1, fetch(zhb_ref), False)

        return c

    lax.fori_loop(0, (qi + 1) // 2, body, 0)
    for hp in range(n_pairs):
        o = jnp.where(low, acc_ref[2 * hp], acc_ref[2 * hp + 1])
        lanes = slice(hp * LANES, (hp + 1) * LANES)
        o_ref[0, :, lanes] = (o * gate_ref[0, :, lanes]).astype(_BF16)


def _prompt_attn_call(sb_bias, q_bf, k_bf, v_bf, gate_b, tri):
    n_seq, t, _ = q_bf.shape
    tq = Q_TILE
    n_heads = ATTN_HEADS_PER_STEP
    width = n_heads * HEAD_DIM
    grid = (n_seq, D_ATTN // width, t // tq)
    tile_spec = pl.BlockSpec((1, tq, width), lambda b, g, qi: (b, qi, g))
    seq_spec = pl.BlockSpec((1, t, width), lambda b, g, qi: (b, 0, g))
    return pl.pallas_call(
        functools.partial(_prompt_attn_kernel, tq=tq, n_heads=n_heads),
        grid=grid,
        in_specs=[
            pl.BlockSpec(memory_space=pltpu.SMEM),
            tile_spec, seq_spec, seq_spec, tile_spec,
            pl.BlockSpec((tq, tq), lambda b, g, qi: (0, 0)),
        ],
        out_specs=tile_spec,
        out_shape=jax.ShapeDtypeStruct((n_seq, t, D_ATTN), _BF16),
        scratch_shapes=[
            pltpu.VMEM((n_heads, tq, 1), _F32),
            pltpu.VMEM((n_heads, tq, LANES), _F32),
            pltpu.VMEM((n_heads, tq, tq), _F32),
            pltpu.VMEM((n_heads, tq, tq), _F32),
        ],
        compiler_params=pltpu.CompilerParams(
            dimension_semantics=("arbitrary", "arbitrary", "arbitrary"),
            vmem_limit_bytes=VMEM_LIMIT),
        name="prompt_attn",
    )(sb_bias, q_bf, k_bf, v_bf, gate_b, tri)


def _sample_attn_kernel(pt_ref, qbd_ref, bias_ref, knew_ref, vnew_ref, *rest, n_g, page):
    del pt_ref
    k_refs = rest[:n_g]
    v_refs = rest[n_g:2 * n_g]
    tri_ref, o_ref, acc_ref, carry_ref = rest[2 * n_g:]
    j = pl.program_id(1)
    nq = SUBLANES
    rows = N_HEADS * nq
    qbd = qbd_ref[0]
    bias = bias_ref[...]
    tri = tri_ref[...]

    @pl.when(j == 0)
    def _():
        pad = jnp.zeros((page - nq, D_ATTN), _F32)
        knew = jnp.concatenate([knew_ref[0], pad], axis=0).astype(_BF16)
        vnew = jnp.concatenate([vnew_ref[0], pad], axis=0).astype(_BF16)
        z = lax.dot_general(qbd, knew, _NT, preferred_element_type=_F32) + bias
        key = lax.broadcasted_iota(jnp.int32, (rows, page), 1)
        qry = lax.broadcasted_iota(jnp.int32, (rows, page), 0) & (nq - 1)
        valid = key < qry
        sp = _softplus(z)
        r = jnp.dot(jnp.where(valid, sp, 0.0).astype(_BF16), tri, preferred_element_type=_F32)
        w = jnp.where(valid, jnp.exp(z - sp + r[:, 0:page]), 0.0)
        acc_ref[...] = jnp.dot(w.astype(_BF16), vnew, preferred_element_type=_F32)
        carry_ref[...] = r[:, page:2 * page]

    zs, sps = [], []
    for g in range(n_g):
        kt = k_refs[g][0, 0].reshape(D_ATTN, page).astype(_BF16)
        z = jnp.dot(qbd, kt, preferred_element_type=_F32) + bias
        zs.append(z)
        sps.append(_softplus(z))
    r = jnp.dot(jnp.concatenate(sps, axis=0).astype(_BF16), tri, preferred_element_type=_F32)
    run = carry_ref[...]
    pv = None
    for g in range(n_g):
        blk = slice(g * rows, (g + 1) * rows)
        w = jnp.exp(zs[g] - sps[g] + (r[blk, 0:page] + run))
        run = run + r[blk, page:2 * page]
        vt = v_refs[g][0, 0].reshape(D_ATTN, page).astype(_BF16)
        c = lax.dot_general(w.astype(_BF16), vt, _NT, preferred_element_type=_F32)
        pv = c if pv is None else pv + c
    carry_ref[...] = run
    acc_ref[...] = acc_ref[...] + pv

    @pl.when(j == pl.num_programs(1) - 1)
    def _():
        acc = acc_ref[...]
        o_ref[0] = jnp.concatenate(
            [acc[h * nq:(h + 1) * nq, h * HEAD_DIM:(h + 1) * HEAD_DIM] for h in range(N_HEADS)], axis=0)


def _sample_attn_call(layer, page_table, qbd, bias_rows, k_new, v_new, cache_kt, cache_vt, tri):
    n_seq, n_pages = page_table.shape
    page = cache_kt.shape[-1]
    n_g = PAGES_PER_STEP
    rows = SUBLANES * N_HEADS
    seq_spec = lambda r, c: pl.BlockSpec((1, r, c), lambda b, j, pt: (b, 0, 0))
    const2 = lambda shape: pl.BlockSpec(shape, lambda b, j, pt: (0, 0))

    def page_spec(g):
        return pl.BlockSpec(
            (1, 1, N_HEADS, HEAD_DIM, page),
            lambda b, j, pt, g=g: (layer, pt[b, n_pages - 1 - (j * n_g + g)], 0, 0, 0))

    page_specs = [page_spec(g) for g in range(n_g)]
    grid_spec = pltpu.PrefetchScalarGridSpec(
        num_scalar_prefetch=1,
        grid=(n_seq, n_pages // n_g),
        in_specs=[seq_spec(rows, D_ATTN), const2((rows, page)),
                  seq_spec(SUBLANES, D_ATTN), seq_spec(SUBLANES, D_ATTN)]
                 + page_specs + page_specs + [const2(tri.shape)],
        out_specs=seq_spec(rows, HEAD_DIM),
        scratch_shapes=[pltpu.VMEM((rows, D_ATTN), _F32), pltpu.VMEM((rows, page), _F32)],
    )
    return pl.pallas_call(
        functools.partial(_sample_attn_kernel, n_g=n_g, page=page),
        grid_spec=grid_spec,
        out_shape=jax.ShapeDtypeStruct((n_seq, rows, HEAD_DIM), _F32),
        compiler_params=pltpu.CompilerParams(
            dimension_semantics=("arbitrary", "arbitrary"),
            vmem_limit_bytes=VMEM_LIMIT),
        name="sample_attn",
    )(page_table, qbd, bias_rows, k_new, v_new, *([cache_kt] * n_g), *([cache_vt] * n_g), tri)


def _outproj_kernel(*refs, nb, tm, gate_inside, final):
    if gate_inside:
        x_ref, gate_ref, ya_ref, yb_ref, gb_ref, yc_ref, w_ref = refs[:7]
        rest = refs[7:]
    else:
        x_ref, gate_ref, ya_ref, yb_ref, yc_ref, w_ref = refs[:6]
        rest = refs[6:]
    rows = nb * tm
    if gate_inside:
        y_b = (yb_ref[...] * gb_ref[...]).astype(_BF16)
    else:
        y_b = yb_ref[...]
    out = (jnp.dot(ya_ref[...].reshape(rows, D_CONV), w_ref[0:D_CONV, :], preferred_element_type=_F32)
           + jnp.dot(y_b.reshape(rows, D_ATTN), w_ref[D_CONV:D_CONV + D_ATTN, :],
                     preferred_element_type=_F32)
           + jnp.dot(yc_ref[...].reshape(rows, D_POOL), w_ref[D_CONV + D_ATTN:, :],
                     preferred_element_type=_F32))
    xn = x_ref[...] + gate_ref[...] * out.reshape(nb, tm, D_MODEL)
    if final:
        fw_ref, y_ref = rest
        ms = jnp.mean(xn * xn, axis=-1, keepdims=True)
        y_ref[...] = xn * lax.rsqrt(ms + RMS_EPS) * fw_ref[...]
    else:
        (y_ref,) = rest
        y_ref[...] = xn


def _outproj_call(x, ada3, y_a, y_b, gate_b, y_c, w_out_bf, final_w, *, nb, tm):
    n_seq, t, _ = x.shape
    grid = (n_seq // nb, t // tm)
    row_spec = lambda width: pl.BlockSpec((nb, tm, width), lambda b, i: (b, i, 0))
    gate_inside = gate_b is not None
    final = final_w is not None
    args = [x, ada3, y_a, y_b]
    in_specs = [row_spec(D_MODEL), pl.BlockSpec((nb, 1, D_MODEL), lambda b, i: (b, 0, 2)),
                row_spec(D_CONV), row_spec(D_ATTN)]
    if gate_inside:
        args.append(gate_b)
        in_specs.append(row_spec(D_ATTN))
    args += [y_c, w_out_bf]
    in_specs += [row_spec(D_POOL), pl.BlockSpec((D_MODEL, D_MODEL), lambda b, i: (0, 0))]
    if final:
        args.append(final_w.reshape(1, D_MODEL))
        in_specs.append(pl.BlockSpec((1, D_MODEL), lambda b, i: (0, 0)))
    return pl.pallas_call(
        functools.partial(_outproj_kernel, nb=nb, tm=tm, gate_inside=gate_inside, final=final),
        grid=grid,
        in_specs=in_specs,
        out_specs=row_spec(D_MODEL),
        out_shape=jax.ShapeDtypeStruct(x.shape, _F32),
        compiler_params=pltpu.CompilerParams(
            dimension_semantics=("arbitrary", "arbitrary"),
            vmem_limit_bytes=VMEM_LIMIT),
        name="outproj",
    )(*args)


def _suffix_matrix(n):
    r = jnp.arange(n)
    tri = jnp.where(r[:, None] > r[None, :], -1.0, 0.0)
    return jnp.concatenate([tri, -jnp.ones((n, n))], axis=1).astype(_BF16)


def kernel(x_prompt, x_sample, cache_k, cache_v, state_conv, state_pool, page_table, c_prompt, c_sample,
           norm_w, w_ada, b_ada, w_in, sb_bias, conv_w, pool_w, pool_scale, w_out, final_norm_w):
    depth = norm_w.shape[0]
    n_prompt, seq, _ = x_prompt.shape
    n_sample, dec_seq, _ = x_sample.shape
    n_pages = page_table.shape[1]
    page = cache_k.shape[2]
    past_len = n_pages * page
    assert dec_seq == SUBLANES and seq % ROW_TILE == 0 and seq % Q_TILE == 0
    assert n_pages % PAGES_PER_STEP == 0 and page == LANES

    w_in_bf = w_in.astype(_BF16)
    w_out_bf = w_out.astype(_BF16)
    eye = jnp.eye(N_HEADS, dtype=_F32)
    pool_w_bd = jnp.einsum('lgcd,gh->lgchd', pool_w, eye[:len(POOL_WINDOWS), :len(POOL_WINDOWS)]).reshape(
        depth, D_POOL, D_POOL).astype(_BF16)
    cache_kt = cache_k.transpose(0, 1, 3, 4, 2)
    cache_vt = cache_v.transpose(0, 1, 3, 4, 2)

    ada = _ada_call(jnp.concatenate([c_prompt, c_sample], axis=0), w_ada, b_ada)
    ada_p = ada[:, :n_prompt].reshape(depth, n_prompt, 1, 3 * D_MODEL)
    ada_s = ada[:, n_prompt:].reshape(depth, n_sample, 1, 3 * D_MODEL)

    tri_prompt = _suffix_matrix(Q_TILE)[:, :Q_TILE]
    tri_page = _suffix_matrix(page)
    head_of_row = jnp.arange(N_HEADS * dec_seq) // dec_seq

    zero_conv = jnp.zeros((n_prompt, CONV_WIDTH - 1, D_CONV), _F32)
    zero_pool = jnp.zeros((n_prompt, POOL_CTX, D_POOL), _F32)

    xp, xs = x_prompt, x_sample
    outs = [[] for _ in range(8)]
    for l in range(depth):
        last = l == depth - 1
        fw = final_norm_w if last else None

        ya, yc, gate_b, qb, kb, vb, k, v, cp, pp = _inproj_call(
            xp, ada_p[l], norm_w[l], w_in_bf[l], zero_conv, zero_pool, conv_w[l], pool_w_bd[l],
            pool_scale[l], nb=1, tm=ROW_TILE, pos0=0, q_scale=0.5 * HEAD_DIM ** -0.5)
        yb = _prompt_attn_call(sb_bias[l], qb, kb, vb, gate_b, tri_prompt)
        xp = _outproj_call(xp, ada_p[l], ya, yb, None, yc, w_out_bf[l], fw, nb=1, tm=ROW_TILE)
        for lst, val in zip(outs[:4], (k, v, cp, pp)):
            lst.append(val)

        ya, yc, gate_b, qb, _, _, k, v, cs, ps = _inproj_call(
            xs, ada_s[l], norm_w[l], w_in_bf[l], state_conv[l], state_pool[l], conv_w[l], pool_w_bd[l],
            pool_scale[l], nb=n_sample, tm=dec_seq, pos0=past_len, q_scale=HEAD_DIM ** -0.5)
        q_hq = qb.reshape(n_sample, dec_seq, N_HEADS, HEAD_DIM).transpose(0, 2, 1, 3)
        qbd = (q_hq[:, :, :, None, :] * eye.astype(_BF16)[None, :, None, :, None]).reshape(
            n_sample, N_HEADS * dec_seq, D_ATTN)
        bias_rows = jnp.broadcast_to(sb_bias[l][head_of_row][:, None], (N_HEADS * dec_seq, page))
        o2 = _sample_attn_call(l, page_table, qbd, bias_rows, k, v, cache_kt, cache_vt, tri_page)
        o_s = o2.reshape(n_sample, N_HEADS, dec_seq, HEAD_DIM).transpose(0, 2, 1, 3).reshape(
            n_sample, dec_seq, D_ATTN)
        xs = _outproj_call(xs, ada_s[l], ya, o_s, gate_b, yc, w_out_bf[l], fw, nb=n_sample, tm=dec_seq)
        for lst, val in zip(outs[4:], (k, v, cs, ps)):
            lst.append(val)

    kp, vp, cpo, ppo, ks, vs, cso, pso = [jnp.stack(o) for o in outs]
    heads = lambda a: a.reshape(a.shape[:-1] + (N_HEADS, HEAD_DIM))
    return (xp, xs, heads(kp), heads(vp), cpo, ppo, heads(ks), heads(vs), cso, pso)
```

```python
import functools

import jax
import jax.numpy as jnp
from jax import lax
from jax.experimental import pallas as pl
from jax.experimental.pallas import tpu as pltpu

D_MODEL = 1024
HEAD_DIM = 64
D_ATTN = 512
N_HEADS = 8
D_CONV = 256
D_POOL = 256
CONV_WIDTH = 3
POOL_WINDOWS = (2, 4, 8, 16)
POOL_GROUP = 64
POOL_CTX = 15
RMS_EPS = 1e-6
LOG2E = 1.4426950408889634
D_IN = 4 * D_CONV + 4 * D_ATTN + 2 * D_POOL
ATTN_COL0 = 4 * D_CONV
POOL_COL0 = ATTN_COL0 + 4 * D_ATTN

LANES = 128
SUBLANES = 8
CONV_HALO = SUBLANES
POOL_HALO = 16
ROW_TILE = 512
Q_TILE = 256
ATTN_HEADS_PER_STEP = 8
PAGES_PER_STEP = 32
VMEM_LIMIT = 52 * 1024 * 1024

_BF16 = jnp.bfloat16
_F32 = jnp.float32
_NT = (((1,), (1,)), ((), ()))


def _silu(x):
    return x * jax.nn.sigmoid(x)


def _softplus(z):
    return jnp.maximum(z, 0.0) + jnp.log(1.0 + jnp.exp2(jnp.abs(z) * (-LOG2E)))


def _ada_kernel(c_ref, w_ref, b_ref, o_ref):
    a = _silu(c_ref[...]).astype(_BF16)
    w = w_ref[0].astype(_BF16)
    o_ref[0] = jnp.dot(a, w, preferred_element_type=_F32) + b_ref[0]


def _ada_call(c_all, w_ada, b_ada):
    depth = w_ada.shape[0]
    n = c_all.shape[0]
    return pl.pallas_call(
        _ada_kernel,
        grid=(depth, 3),
        in_specs=[
            pl.BlockSpec((n, D_MODEL), lambda l, j: (0, 0)),
            pl.BlockSpec((1, D_MODEL, D_MODEL), lambda l, j: (l, 0, j)),
            pl.BlockSpec((1, 1, D_MODEL), lambda l, j: (l, 0, j)),
        ],
        out_specs=pl.BlockSpec((1, n, D_MODEL), lambda l, j: (l, 0, j)),
        out_shape=jax.ShapeDtypeStruct((depth, n, 3 * D_MODEL), _F32),
        compiler_params=pltpu.CompilerParams(
            dimension_semantics=("arbitrary", "arbitrary"),
            vmem_limit_bytes=VMEM_LIMIT),
        name="ada",
    )(c_all, w_ada, b_ada.reshape(depth, 1, 3 * D_MODEL))


def _inproj_kernel(x_ref, shift_ref, scale_ref, normw_ref, w_ref, convprev_ref, poolprev_ref,
                   convw_ref, poolw_ref, pscale_ref, *rest,
                   nb, tm, n_tiles, pos0, q_scale, kv_transposed, n_aliased):
    (ya_ref, yc_ref, gateb_ref, qb_ref, kb_ref, vb_ref, k_ref, v_ref,
     newconv_ref, newpool_ref, extc_ref, extu_ref) = rest[n_aliased:]
    i = pl.program_id(1)
    rows = nb * tm

    @pl.when(i == 0)
    def _():
        extc_ref[:, CONV_HALO - 2:CONV_HALO, :] = convprev_ref[...]
        extu_ref[:, POOL_HALO - POOL_CTX:POOL_HALO, :] = poolprev_ref[...]

    def normed(seqs, rws):
        x = x_ref[seqs, rws, :]
        ms = jnp.mean(x * x, axis=-1, keepdims=True)
        xn = x * lax.rsqrt(ms + RMS_EPS) * normw_ref[...]
        h = xn * (1.0 + scale_ref[seqs]) + shift_ref[seqs]
        return h.reshape(rows // 2, D_MODEL).astype(_BF16)

    if nb == 1:
        halves = [(slice(0, 1), slice(0, tm // 2)), (slice(0, 1), slice(tm // 2, tm))]
    else:
        halves = [(slice(0, nb // 2), slice(0, tm)), (slice(nb // 2, nb), slice(0, tm))]
    h_top = normed(*halves[0])
    pc_top = jnp.dot(h_top, w_ref[:, 0:ATTN_COL0], preferred_element_type=_F32)
    h_bot = normed(*halves[1])
    pc_bot = jnp.dot(h_bot, w_ref[:, 0:ATTN_COL0], preferred_element_type=_F32)
    h2 = jnp.concatenate([h_top, h_bot], axis=0)
    pc = jnp.concatenate([pc_top, pc_bot], axis=0)
    pp = jnp.dot(h2, w_ref[:, POOL_COL0:D_IN], preferred_element_type=_F32)
    pa = jnp.dot(h2, w_ref[:, ATTN_COL0:POOL_COL0], preferred_element_type=_F32)

    a_b = pc[:, 0:D_CONV].reshape(nb, tm, D_CONV)
    a_c = pc[:, D_CONV:2 * D_CONV]
    a_h = pc[:, 2 * D_CONV:3 * D_CONV]
    a_z = pc[:, 3 * D_CONV:4 * D_CONV].reshape(nb, tm, D_CONV)
    conv_in = (a_c * a_h).reshape(nb, tm, D_CONV)
    extc_ref[:, CONV_HALO:CONV_HALO + tm, :] = conv_in
    cw = convw_ref[...]
    conv_out = (extc_ref[:, CONV_HALO - 2:CONV_HALO - 2 + tm, :] * cw[0:1]
                + extc_ref[:, CONV_HALO - 1:CONV_HALO - 1 + tm, :] * cw[1:2]
                + conv_in * cw[2:3])
    ya_ref[...] = (a_b * conv_out * _silu(a_z)).astype(_BF16)
    last_conv = extc_ref[:, CONV_HALO + tm - 2:CONV_HALO + tm, :]
    newconv_ref[...] = last_conv
    if n_tiles > 1:
        extc_ref[:, CONV_HALO - 2:CONV_HALO, :] = last_conv

    p_u = pp[:, 0:D_POOL].reshape(nb, tm, D_POOL)
    p_z = pp[:, D_POOL:2 * D_POOL].reshape(nb, tm, D_POOL)
    extu_ref[:, POOL_HALO:POOL_HALO + tm, :] = p_u

    def win_sum(lo, hi, lane0):
        acc = extu_ref[:, POOL_HALO - lo:POOL_HALO - lo + tm, lane0:lane0 + LANES]
        for kk in range(lo + 1, hi):
            acc = acc + extu_ref[:, POOL_HALO - kk:POOL_HALO - kk + tm, lane0:lane0 + LANES]
        return acc

    s2 = win_sum(0, 2, 0)
    s4 = s2 + win_sum(2, 4, 0)
    s8 = win_sum(0, 8, LANES)
    s16 = s8 + win_sum(8, 16, LANES)
    pos = pos0 + i * tm + lax.broadcasted_iota(jnp.int32, (nb, tm, LANES), 1)
    lane = lax.broadcasted_iota(jnp.int32, (nb, tm, LANES), 2)
    first = lane < POOL_GROUP

    def cnt(win):
        return jnp.minimum(win, pos + 1).astype(_F32)

    mean_lo = jnp.where(first, s2 / cnt(2), s4 / cnt(4))
    mean_hi = jnp.where(first, s8 / cnt(8), s16 / cnt(16))
    pooled = jnp.concatenate([mean_lo, mean_hi], axis=-1) - p_u
    y_c = jnp.dot(pooled.reshape(rows, D_POOL).astype(_BF16), poolw_ref[...],
                  preferred_element_type=_F32).reshape(nb, tm, D_POOL)
    yc_ref[...] = (y_c * pscale_ref[...] * _silu(p_z)).astype(_BF16)
    last_pool = extu_ref[:, POOL_HALO + tm - POOL_CTX:POOL_HALO + tm, :]
    newpool_ref[...] = last_pool
    if n_tiles > 1:
        extu_ref[:, POOL_HALO - POOL_CTX:POOL_HALO, :] = last_pool

    q = pa[:, 0:D_ATTN].reshape(nb, tm, D_ATTN)
    k = pa[:, D_ATTN:2 * D_ATTN].reshape(nb, tm, D_ATTN)
    v = pa[:, 2 * D_ATTN:3 * D_ATTN].reshape(nb, tm, D_ATTN)
    b_z = pa[:, 3 * D_ATTN:4 * D_ATTN].reshape(nb, tm, D_ATTN)
    qb_ref[...] = (q * q_scale).astype(_BF16)
    if kv_transposed:
        k_ref[0, 0] = pa[:, D_ATTN:2 * D_ATTN].T
        v_ref[0, 0] = pa[:, 2 * D_ATTN:3 * D_ATTN].T
    else:
        k_ref[...] = k
        v_ref[...] = v
    kb_ref[...] = k.astype(_BF16)
    vb_ref[...] = v.astype(_BF16)
    gateb_ref[...] = _silu(b_z)


def _inproj_call(x, ada3, norm_w, w_in_bf, conv_prev, pool_prev, conv_w, pool_w_bd, pool_scale,
                 *, nb, tm, pos0, q_scale, kv_stack=None):
    n_seq, t, _ = x.shape
    n_tiles = t // tm
    grid = (n_seq // nb, n_tiles)
    kv_transposed = kv_stack is not None
    aliased = []
    if kv_transposed:
        layer, depth, k_buf, v_buf = kv_stack
        assert nb == 1
        aliased = [] if k_buf is None else [k_buf, v_buf]
        kv_shape = jax.ShapeDtypeStruct((depth, n_seq, D_ATTN, t), _F32)
        kv_spec = pl.BlockSpec((1, 1, D_ATTN, tm), lambda b, i: (layer, b, 0, i))
    else:
        kv_shape = jax.ShapeDtypeStruct((n_seq, t, D_ATTN), _F32)
        kv_spec = pl.BlockSpec((nb, tm, D_ATTN), lambda b, i: (b, i, 0))
    row_spec = lambda width: pl.BlockSpec((nb, tm, width), lambda b, i: (b, i, 0))
    ada_spec = lambda j: pl.BlockSpec((nb, 1, D_MODEL), lambda b, i, j=j: (b, 0, j))
    const2 = lambda shape: pl.BlockSpec(shape, lambda b, i: (0, 0))
    state_spec = lambda r, c: pl.BlockSpec((nb, r, c), lambda b, i: (b, 0, 0))
    sds = jax.ShapeDtypeStruct
    out_shape = (
        sds((n_seq, t, D_CONV), _BF16), sds((n_seq, t, D_POOL), _BF16),
        sds((n_seq, t, D_ATTN), _F32),
        sds((n_seq, t, D_ATTN), _BF16), sds((n_seq, t, D_ATTN), _BF16), sds((n_seq, t, D_ATTN), _BF16),
        kv_shape, kv_shape,
        sds((n_seq, CONV_WIDTH - 1, D_CONV), _F32), sds((n_seq, POOL_CTX, D_POOL), _F32),
    )
    out_specs = (
        row_spec(D_CONV), row_spec(D_POOL), row_spec(D_ATTN),
        row_spec(D_ATTN), row_spec(D_ATTN), row_spec(D_ATTN),
        kv_spec, kv_spec,
        state_spec(CONV_WIDTH - 1, D_CONV), state_spec(POOL_CTX, D_POOL),
    )
    n_in = 10
    return pl.pallas_call(
        functools.partial(_inproj_kernel, nb=nb, tm=tm, n_tiles=n_tiles, pos0=pos0, q_scale=q_scale,
                          kv_transposed=kv_transposed, n_aliased=len(aliased)),
        grid=grid,
        in_specs=[
            row_spec(D_MODEL), ada_spec(0), ada_spec(1),
            const2((1, D_MODEL)), const2((D_MODEL, D_IN)),
            state_spec(CONV_WIDTH - 1, D_CONV), state_spec(POOL_CTX, D_POOL),
            const2((CONV_WIDTH, D_CONV)), const2((D_POOL, D_POOL)), const2((1, D_POOL)),
        ] + [pl.BlockSpec(memory_space=pl.ANY)] * len(aliased),
        input_output_aliases={n_in + j: 6 + j for j in range(len(aliased))},
        out_specs=out_specs,
        out_shape=out_shape,
        scratch_shapes=[
            pltpu.VMEM((nb, CONV_HALO + tm, D_CONV), _F32),
            pltpu.VMEM((nb, POOL_HALO + tm, D_POOL), _F32),
        ],
        compiler_params=pltpu.CompilerParams(
            dimension_semantics=("arbitrary", "arbitrary"),
            vmem_limit_bytes=VMEM_LIMIT),
        name="inproj",
    )(x, ada3, ada3, norm_w.reshape(1, D_MODEL), w_in_bf, conv_prev, pool_prev,
      conv_w, pool_w_bd, pool_scale.reshape(1, D_POOL), *aliased)


def _prompt_attn_kernel(bias_ref, q_ref, k_ref, v_ref, gate_ref, tri_ref, o_ref,
                        carry_ref, acc_ref, zha_ref, zhb_ref, *, tq, n_heads):
    g = pl.program_id(1)
    qi = pl.program_id(2)
    n_pairs = n_heads // 2
    lane = lax.broadcasted_iota(jnp.int32, (tq, LANES), 1)
    low = lane < HEAD_DIM
    qm, half_bias = [], []
    for hp in range(n_pairs):
        q2 = q_ref[0, :, hp * LANES:(hp + 1) * LANES]
        zero = jnp.zeros_like(q2)
        qm += [jnp.where(low, q2, zero), jnp.where(low, zero, q2)]
        half_bias += [0.5 * bias_ref[g * n_heads + 2 * hp], 0.5 * bias_ref[g * n_heads + 2 * hp + 1]]
    tri = tri_ref[...]
    row = lax.broadcasted_iota(jnp.int32, (tq, tq), 0)
    col = lax.broadcasted_iota(jnp.int32, (tq, tq), 1)
    below_diag = col < row

    pair_lanes = lambda h: slice((h // 2) * LANES, (h // 2 + 1) * LANES)

    def half_scores(kb):
        start = pl.multiple_of(kb * tq, tq)
        out = []
        for h in range(n_heads):
            k2 = k_ref[0, pl.ds(start, tq), pair_lanes(h)]
            out.append(lax.dot_general(qm[h], k2, _NT, preferred_element_type=_F32) + half_bias[h])
        return tuple(out)

    def block(kb, zhs, diagonal):
        start = pl.multiple_of(kb * tq, tq)
        sps, totals, logsigs, ws = [], [], [], []
        for h in range(n_heads):
            zh = zhs[h]
            ah = jnp.abs(zh)
            l1p = jnp.log(1.0 + jnp.exp2(ah * (-2.0 * LOG2E)))
            sp = (zh + ah) + l1p
            if diagonal:
                sp = jnp.where(below_diag, sp, 0.0)
            sps.append(sp.astype(_BF16))
            totals.append(-jnp.sum(sp, axis=1, keepdims=True))
            logsigs.append((zh - ah) - l1p)
        for h in range(n_heads):
            within = jnp.dot(sps[h], tri, preferred_element_type=_F32)
            w = jnp.exp(logsigs[h] + within)
            if diagonal:
                w = jnp.where(below_diag, w, 0.0)
            ws.append(w.astype(_BF16))
        for h in range(n_heads):
            v2 = v_ref[0, pl.ds(start, tq), pair_lanes(h)]
            pv = jnp.dot(ws[h], v2, preferred_element_type=_F32)
            if diagonal:
                acc_ref[h] = pv
                carry_ref[h] = totals[h]
            else:
                carry = carry_ref[h]
                acc_ref[h] = acc_ref[h] + pv * jnp.exp(carry)
                carry_ref[h] = carry + totals[h]

    def stash(ref, kb):
        for h, zh in enumerate(half_scores(jnp.maximum(kb, 0))):
            ref[h] = zh

    def fetch(ref):
        return tuple(ref[h] for h in range(n_heads))

    stash(zha_ref, qi - 1)
    block(qi, half_scores(qi), True)

    def body(pair, c):
        kb = qi - 1 - 2 * pair
        stash(zhb_ref, kb - 1)
        block(kb, fetch(zha_ref), False)

        @pl.when(kb >= 1)
        def _():
            stash(zha_ref, kb - 2)
            block(kb - 1, fetch(zhb_ref), False)

        return c

    lax.fori_loop(0, (qi + 1) // 2, body, 0)
    for hp in range(n_pairs):
        o = jnp.where(low, acc_ref[2 * hp], acc_ref[2 * hp + 1])
        lanes = slice(hp * LANES, (hp + 1) * LANES)
        o_ref[0, :, lanes] = (o * gate_ref[0, :, lanes]).astype(_BF16)


def _prompt_attn_call(sb_bias, q_bf, k_bf, v_bf, gate_b, tri):
    n_seq, t, _ = q_bf.shape
    tq = Q_TILE
    n_heads = ATTN_HEADS_PER_STEP
    width = n_heads * HEAD_DIM
    grid = (n_seq, D_ATTN // width, t // tq)
    tile_spec = pl.BlockSpec((1, tq, width), lambda b, g, qi: (b, qi, g))
    seq_spec = pl.BlockSpec((1, t, width), lambda b, g, qi: (b, 0, g))
    return pl.pallas_call(
        functools.partial(_prompt_attn_kernel, tq=tq, n_heads=n_heads),
        grid=grid,
        in_specs=[
            pl.BlockSpec(memory_space=pltpu.SMEM),
            tile_spec, seq_spec, seq_spec, tile_spec,
            pl.BlockSpec((tq, tq), lambda b, g, qi: (0, 0)),
        ],
        out_specs=tile_spec,
        out_shape=jax.ShapeDtypeStruct((n_seq, t, D_ATTN), _BF16),
        scratch_shapes=[
            pltpu.VMEM((n_heads, tq, 1), _F32),
            pltpu.VMEM((n_heads, tq, LANES), _F32),
            pltpu.VMEM((n_heads, tq, tq), _F32),
            pltpu.VMEM((n_heads, tq, tq), _F32),
        ],
        compiler_params=pltpu.CompilerParams(
            dimension_semantics=("arbitrary", "arbitrary", "arbitrary"),
            vmem_limit_bytes=VMEM_LIMIT),
        name="prompt_attn",
    )(sb_bias, q_bf, k_bf, v_bf, gate_b, tri)


def _sample_attn_kernel(pt_ref, qbd_ref, bias_ref, knew_ref, vnew_ref, *rest, n_g, page):
    del pt_ref
    k_refs = rest[:n_g]
    v_refs = rest[n_g:2 * n_g]
    tri_ref, o_ref, acc_ref, carry_ref = rest[2 * n_g:]
    j = pl.program_id(1)
    nq = SUBLANES
    rows = N_HEADS * nq
    qbd = qbd_ref[0]
    bias = bias_ref[...]
    tri = tri_ref[...]

    @pl.when(j == 0)
    def _():
        pad = jnp.zeros((page - nq, D_ATTN), _F32)
        knew = jnp.concatenate([knew_ref[0], pad], axis=0).astype(_BF16)
        vnew = jnp.concatenate([vnew_ref[0], pad], axis=0).astype(_BF16)
        z = lax.dot_general(qbd, knew, _NT, preferred_element_type=_F32) + bias
        key = lax.broadcasted_iota(jnp.int32, (rows, page), 1)
        qry = lax.broadcasted_iota(jnp.int32, (rows, page), 0) & (nq - 1)
        valid = key < qry
        sp = _softplus(z)
        r = jnp.dot(jnp.where(valid, sp, 0.0).astype(_BF16), tri, preferred_element_type=_F32)
        w = jnp.where(valid, jnp.exp(z - sp + r[:, 0:page]), 0.0)
        acc_ref[...] = jnp.dot(w.astype(_BF16), vnew, preferred_element_type=_F32)
        carry_ref[...] = r[:, page:2 * page]

    zs, sps = [], []
    for g in range(n_g):
        kt = k_refs[g][0, 0].reshape(D_ATTN, page).astype(_BF16)
        z = jnp.dot(qbd, kt, preferred_element_type=_F32) + bias
        zs.append(z)
        sps.append(_softplus(z))
    r = jnp.dot(jnp.concatenate(sps, axis=0).astype(_BF16), tri, preferred_element_type=_F32)
    run = carry_ref[...]
    pv = None
    for g in range(n_g):
        blk = slice(g * rows, (g + 1) * rows)
        w = jnp.exp(zs[g] - sps[g] + (r[blk, 0:page] + run))
        run = run + r[blk, page:2 * page]
        vt = v_refs[g][0, 0].reshape(D_ATTN, page).astype(_BF16)
        c = lax.dot_general(w.astype(_BF16), vt, _NT, preferred_element_type=_F32)
        pv = c if pv is None else pv + c
    carry_ref[...] = run
    acc_ref[...] = acc_ref[...] + pv

    @pl.when(j == pl.num_programs(1) - 1)
    def _():
        acc = acc_ref[...]
        o_ref[0] = jnp.concatenate(
            [acc[h * nq:(h + 1) * nq, h * HEAD_DIM:(h + 1) * HEAD_DIM] for h in range(N_HEADS)], axis=0)


def _sample_attn_call(layer, page_table, qbd, bias_rows, k_new, v_new, cache_kt, cache_vt, tri):
    n_seq, n_pages = page_table.shape
    page = cache_kt.shape[-1]
    n_g = PAGES_PER_STEP
    rows = SUBLANES * N_HEADS
    seq_spec = lambda r, c: pl.BlockSpec((1, r, c), lambda b, j, pt: (b, 0, 0))
    const2 = lambda shape: pl.BlockSpec(shape, lambda b, j, pt: (0, 0))

    def page_spec(g):
        return pl.BlockSpec(
            (1, 1, N_HEADS, HEAD_DIM, page),
            lambda b, j, pt, g=g: (layer, pt[b, n_pages - 1 - (j * n_g + g)], 0, 0, 0))

    page_specs = [page_spec(g) for g in range(n_g)]
    grid_spec = pltpu.PrefetchScalarGridSpec(
        num_scalar_prefetch=1,
        grid=(n_seq, n_pages // n_g),
        in_specs=[seq_spec(rows, D_ATTN), const2((rows, page)),
                  seq_spec(SUBLANES, D_ATTN), seq_spec(SUBLANES, D_ATTN)]
                 + page_specs + page_specs + [const2(tri.shape)],
        out_specs=seq_spec(rows, HEAD_DIM),
        scratch_shapes=[pltpu.VMEM((rows, D_ATTN), _F32), pltpu.VMEM((rows, page), _F32)],
    )
    return pl.pallas_call(
        functools.partial(_sample_attn_kernel, n_g=n_g, page=page),
        grid_spec=grid_spec,
        out_shape=jax.ShapeDtypeStruct((n_seq, rows, HEAD_DIM), _F32),
        compiler_params=pltpu.CompilerParams(
            dimension_semantics=("arbitrary", "arbitrary"),
            vmem_limit_bytes=VMEM_LIMIT),
        name="sample_attn",
    )(page_table, qbd, bias_rows, k_new, v_new, *([cache_kt] * n_g), *([cache_vt] * n_g), tri)


def _outproj_kernel(*refs, nb, tm, gate_inside, final):
    if gate_inside:
        x_ref, gate_ref, ya_ref, yb_ref, gb_ref, yc_ref, w_ref = refs[:7]
        rest = refs[7:]
    else:
        x_ref, gate_ref, ya_ref, yb_ref, yc_ref, w_ref = refs[:6]
        rest = refs[6:]
    rows = nb * tm
    if gate_inside:
        y_b = (yb_ref[...] * gb_ref[...]).astype(_BF16)
    else:
        y_b = yb_ref[...]
    out = (jnp.dot(ya_ref[...].reshape(rows, D_CONV), w_ref[0:D_CONV, :], preferred_element_type=_F32)
           + jnp.dot(y_b.reshape(rows, D_ATTN), w_ref[D_CONV:D_CONV + D_ATTN, :],
                     preferred_element_type=_F32)
           + jnp.dot(yc_ref[...].reshape(rows, D_POOL), w_ref[D_CONV + D_ATTN:, :],
                     preferred_element_type=_F32))
    xn = x_ref[...] + gate_ref[...] * out.reshape(nb, tm, D_MODEL)
    if final:
        fw_ref, y_ref = rest
        ms = jnp.mean(xn * xn, axis=-1, keepdims=True)
        y_ref[...] = xn * lax.rsqrt(ms + RMS_EPS) * fw_ref[...]
    else:
        (y_ref,) = rest
        y_ref[...] = xn


def _outproj_call(x, ada3, y_a, y_b, gate_b, y_c, w_out_bf, final_w, *, nb, tm):
    n_seq, t, _ = x.shape
    grid = (n_seq // nb, t // tm)
    row_spec = lambda width: pl.BlockSpec((nb, tm, width), lambda b, i: (b, i, 0))
    gate_inside = gate_b is not None
    final = final_w is not None
    args = [x, ada3, y_a, y_b]
    in_specs = [row_spec(D_MODEL), pl.BlockSpec((nb, 1, D_MODEL), lambda b, i: (b, 0, 2)),
                row_spec(D_CONV), row_spec(D_ATTN)]
    if gate_inside:
        args.append(gate_b)
        in_specs.append(row_spec(D_ATTN))
    args += [y_c, w_out_bf]
    in_specs += [row_spec(D_POOL), pl.BlockSpec((D_MODEL, D_MODEL), lambda b, i: (0, 0))]
    if final:
        args.append(final_w.reshape(1, D_MODEL))
        in_specs.append(pl.BlockSpec((1, D_MODEL), lambda b, i: (0, 0)))
    return pl.pallas_call(
        functools.partial(_outproj_kernel, nb=nb, tm=tm, gate_inside=gate_inside, final=final),
        grid=grid,
        in_specs=in_specs,
        out_specs=row_spec(D_MODEL),
        out_shape=jax.ShapeDtypeStruct(x.shape, _F32),
        compiler_params=pltpu.CompilerParams(
            dimension_semantics=("arbitrary", "arbitrary"),
            vmem_limit_bytes=VMEM_LIMIT),
        name="outproj",
    )(*args)


def _suffix_matrix(n):
    r = jnp.arange(n)
    tri = jnp.where(r[:, None] > r[None, :], -1.0, 0.0)
    return jnp.concatenate([tri, -jnp.ones((n, n))], axis=1).astype(_BF16)


def kernel(x_prompt, x_sample, cache_k, cache_v, state_conv, state_pool, page_table, c_prompt, c_sample,
           norm_w, w_ada, b_ada, w_in, sb_bias, conv_w, pool_w, pool_scale, w_out, final_norm_w):
    depth = norm_w.shape[0]
    n_prompt, seq, _ = x_prompt.shape
    n_sample, dec_seq, _ = x_sample.shape
    n_pages = page_table.shape[1]
    page = cache_k.shape[2]
    past_len = n_pages * page
    assert dec_seq == SUBLANES and seq % ROW_TILE == 0 and seq % Q_TILE == 0
    assert n_pages % PAGES_PER_STEP == 0 and page == LANES

    w_in_bf = w_in.astype(_BF16)
    w_out_bf = w_out.astype(_BF16)
    eye = jnp.eye(N_HEADS, dtype=_F32)
    pool_w_bd = jnp.einsum('lgcd,gh->lgchd', pool_w, eye[:len(POOL_WINDOWS), :len(POOL_WINDOWS)]).reshape(
        depth, D_POOL, D_POOL).astype(_BF16)
    cache_kt = cache_k.transpose(0, 1, 3, 4, 2)
    cache_vt = cache_v.transpose(0, 1, 3, 4, 2)

    ada = _ada_call(jnp.concatenate([c_prompt, c_sample], axis=0), w_ada, b_ada)
    ada_p = ada[:, :n_prompt].reshape(depth, n_prompt, 1, 3 * D_MODEL)
    ada_s = ada[:, n_prompt:].reshape(depth, n_sample, 1, 3 * D_MODEL)

    tri_prompt = _suffix_matrix(Q_TILE)[:, :Q_TILE]
    tri_page = _suffix_matrix(page)
    head_of_row = jnp.arange(N_HEADS * dec_seq) // dec_seq

    zero_conv = jnp.zeros((n_prompt, CONV_WIDTH - 1, D_CONV), _F32)
    zero_pool = jnp.zeros((n_prompt, POOL_CTX, D_POOL), _F32)

    xp, xs = x_prompt, x_sample
    outs = [[] for _ in range(6)]
    kp_t = vp_t = None
    for l in range(depth):
        last = l == depth - 1
        fw = final_norm_w if last else None

        ya, yc, gate_b, qb, kb, vb, kp_t, vp_t, cp, pp = _inproj_call(
            xp, ada_p[l], norm_w[l], w_in_bf[l], zero_conv, zero_pool, conv_w[l], pool_w_bd[l],
            pool_scale[l], nb=1, tm=ROW_TILE, pos0=0, q_scale=0.5 * HEAD_DIM ** -0.5,
            kv_stack=(l, depth, kp_t, vp_t))
        yb = _prompt_attn_call(sb_bias[l], qb, kb, vb, gate_b, tri_prompt)
        xp = _outproj_call(xp, ada_p[l], ya, yb, None, yc, w_out_bf[l], fw, nb=1, tm=ROW_TILE)
        for lst, val in zip(outs[:2], (cp, pp)):
            lst.append(val)

        ya, yc, gate_b, qb, _, _, k, v, cs, ps = _inproj_call(
            xs, ada_s[l], norm_w[l], w_in_bf[l], state_conv[l], state_pool[l], conv_w[l], pool_w_bd[l],
            pool_scale[l], nb=n_sample, tm=dec_seq, pos0=past_len, q_scale=HEAD_DIM ** -0.5)
        q_hq = qb.reshape(n_sample, dec_seq, N_HEADS, HEAD_DIM).transpose(0, 2, 1, 3)
        qbd = (q_hq[:, :, :, None, :] * eye.astype(_BF16)[None, :, None, :, None]).reshape(
            n_sample, N_HEADS * dec_seq, D_ATTN)
        bias_rows = jnp.broadcast_to(sb_bias[l][head_of_row][:, None], (N_HEADS * dec_seq, page))
        o2 = _sample_attn_call(l, page_table, qbd, bias_rows, k, v, cache_kt, cache_vt, tri_page)
        o_s = o2.reshape(n_sample, N_HEADS, dec_seq, HEAD_DIM).transpose(0, 2, 1, 3).reshape(
            n_sample, dec_seq, D_ATTN)
        xs = _outproj_call(xs, ada_s[l], ya, o_s, gate_b, yc, w_out_bf[l], fw, nb=n_sample, tm=dec_seq)
        for lst, val in zip(outs[2:], (k, v, cs, ps)):
            lst.append(val)

    cpo, ppo, ks, vs, cso, pso = [jnp.stack(o) for o in outs]
    heads = lambda a: a.reshape(a.shape[:-1] + (N_HEADS, HEAD_DIM))
    to_rows = lambda a: a.reshape(depth, n_prompt, N_HEADS, HEAD_DIM, seq).transpose(0, 1, 4, 2, 3)
    return (xp, xs, to_rows(kp_t), to_rows(vp_t), cpo, ppo, heads(ks), heads(vs), cso, pso)
```

```python
import functools

import jax
import jax.numpy as jnp
from jax import lax
from jax.experimental import pallas as pl
from jax.experimental.pallas import tpu as pltpu

D_MODEL = 1024
HEAD_DIM = 64
D_ATTN = 512
N_HEADS = 8
D_CONV = 256
D_POOL = 256
CONV_WIDTH = 3
POOL_WINDOWS = (2, 4, 8, 16)
POOL_GROUP = 64
POOL_CTX = 15
RMS_EPS = 1e-6
LOG2E = 1.4426950408889634
D_IN = 4 * D_CONV + 4 * D_ATTN + 2 * D_POOL
ATTN_COL0 = 4 * D_CONV
POOL_COL0 = ATTN_COL0 + 4 * D_ATTN

LANES = 128
SUBLANES = 8
CONV_HALO = SUBLANES
POOL_HALO = 16
ROW_TILE = 512
Q_TILE = 256
ATTN_HEADS_PER_STEP = 8
PAGES_PER_STEP = 32
VMEM_LIMIT = 52 * 1024 * 1024

_BF16 = jnp.bfloat16
_F32 = jnp.float32
_NT = (((1,), (1,)), ((), ()))


def _silu(x):
    return x * jax.nn.sigmoid(x)


def _softplus(z):
    return jnp.maximum(z, 0.0) + jnp.log(1.0 + jnp.exp2(jnp.abs(z) * (-LOG2E)))


def _ada_kernel(cp_ref, cs_ref, w_ref, b_ref, op_ref, os_ref):
    w = w_ref[0].astype(_BF16)
    for c_ref, o_ref in ((cp_ref, op_ref), (cs_ref, os_ref)):
        a = _silu(c_ref[...]).astype(_BF16)
        o_ref[0] = jnp.dot(a, w, preferred_element_type=_F32) + b_ref[0]


def _ada_call(c_prompt, c_sample, w_ada, b_ada):
    depth = w_ada.shape[0]
    n_p, n_s = c_prompt.shape[0], c_sample.shape[0]
    out_spec = lambda n: pl.BlockSpec((1, n, D_MODEL), lambda l, j: (l, 0, j))
    return pl.pallas_call(
        _ada_kernel,
        grid=(depth, 3),
        in_specs=[
            pl.BlockSpec((n_p, D_MODEL), lambda l, j: (0, 0)),
            pl.BlockSpec((n_s, D_MODEL), lambda l, j: (0, 0)),
            pl.BlockSpec((1, D_MODEL, D_MODEL), lambda l, j: (l, 0, j)),
            pl.BlockSpec((1, 1, D_MODEL), lambda l, j: (l, 0, j)),
        ],
        out_specs=(out_spec(n_p), out_spec(n_s)),
        out_shape=(jax.ShapeDtypeStruct((depth, n_p, 3 * D_MODEL), _F32),
                   jax.ShapeDtypeStruct((depth, n_s, 3 * D_MODEL), _F32)),
        compiler_params=pltpu.CompilerParams(
            dimension_semantics=("arbitrary", "arbitrary"),
            vmem_limit_bytes=VMEM_LIMIT),
        name="ada",
    )(c_prompt, c_sample, w_ada, b_ada.reshape(depth, 1, 3 * D_MODEL))


def _inproj_kernel(x_ref, shift_ref, scale_ref, normw_ref, w_ref, convprev_ref, poolprev_ref,
                   convw_ref, poolw_ref, pscale_ref, *rest,
                   nb, tm, n_tiles, pos0, q_scale, kv_transposed, n_aliased):
    (ya_ref, yc_ref, gateb_ref, qb_ref, kb_ref, vb_ref, k_ref, v_ref,
     newconv_ref, newpool_ref, extc_ref, extu_ref) = rest[n_aliased:]
    i = pl.program_id(1)
    rows = nb * tm

    @pl.when(i == 0)
    def _():
        extc_ref[:, CONV_HALO - 2:CONV_HALO, :] = convprev_ref[...]
        extu_ref[:, POOL_HALO - POOL_CTX:POOL_HALO, :] = poolprev_ref[...]

    def normed(seqs, rws):
        x = x_ref[seqs, rws, :]
        ms = jnp.mean(x * x, axis=-1, keepdims=True)
        xn = x * lax.rsqrt(ms + RMS_EPS) * normw_ref[...]
        h = xn * (1.0 + scale_ref[seqs]) + shift_ref[seqs]
        return h.reshape(rows // 2, D_MODEL).astype(_BF16)

    if nb == 1:
        halves = [(slice(0, 1), slice(0, tm // 2)), (slice(0, 1), slice(tm // 2, tm))]
    else:
        halves = [(slice(0, nb // 2), slice(0, tm)), (slice(nb // 2, nb), slice(0, tm))]
    h_top = normed(*halves[0])
    pc_top = jnp.dot(h_top, w_ref[0, :, 0:ATTN_COL0], preferred_element_type=_F32)
    h_bot = normed(*halves[1])
    pc_bot = jnp.dot(h_bot, w_ref[0, :, 0:ATTN_COL0], preferred_element_type=_F32)
    h2 = jnp.concatenate([h_top, h_bot], axis=0)
    pc = jnp.concatenate([pc_top, pc_bot], axis=0)
    pp = jnp.dot(h2, w_ref[0, :, POOL_COL0:D_IN], preferred_element_type=_F32)
    pa = jnp.dot(h2, w_ref[0, :, ATTN_COL0:POOL_COL0], preferred_element_type=_F32)

    a_b = pc[:, 0:D_CONV].reshape(nb, tm, D_CONV)
    a_c = pc[:, D_CONV:2 * D_CONV]
    a_h = pc[:, 2 * D_CONV:3 * D_CONV]
    a_z = pc[:, 3 * D_CONV:4 * D_CONV].reshape(nb, tm, D_CONV)
    conv_in = (a_c * a_h).reshape(nb, tm, D_CONV)
    extc_ref[:, CONV_HALO:CONV_HALO + tm, :] = conv_in
    cw = convw_ref[...]
    conv_out = (extc_ref[:, CONV_HALO - 2:CONV_HALO - 2 + tm, :] * cw[0:1]
                + extc_ref[:, CONV_HALO - 1:CONV_HALO - 1 + tm, :] * cw[1:2]
                + conv_in * cw[2:3])
    ya_ref[...] = (a_b * conv_out * _silu(a_z)).astype(_BF16)
    last_conv = extc_ref[:, CONV_HALO + tm - 2:CONV_HALO + tm, :]
    newconv_ref[...] = last_conv
    if n_tiles > 1:
        extc_ref[:, CONV_HALO - 2:CONV_HALO, :] = last_conv

    p_u = pp[:, 0:D_POOL].reshape(nb, tm, D_POOL)
    p_z = pp[:, D_POOL:2 * D_POOL].reshape(nb, tm, D_POOL)
    extu_ref[:, POOL_HALO:POOL_HALO + tm, :] = p_u

    def win_sum(lo, hi, lane0):
        acc = extu_ref[:, POOL_HALO - lo:POOL_HALO - lo + tm, lane0:lane0 + LANES]
        for kk in range(lo + 1, hi):
            acc = acc + extu_ref[:, POOL_HALO - kk:POOL_HALO - kk + tm, lane0:lane0 + LANES]
        return acc

    s2 = win_sum(0, 2, 0)
    s4 = s2 + win_sum(2, 4, 0)
    s8 = win_sum(0, 8, LANES)
    s16 = s8 + win_sum(8, 16, LANES)
    pos = pos0 + i * tm + lax.broadcasted_iota(jnp.int32, (nb, tm, LANES), 1)
    lane = lax.broadcasted_iota(jnp.int32, (nb, tm, LANES), 2)
    first = lane < POOL_GROUP

    def cnt(win):
        return jnp.minimum(win, pos + 1).astype(_F32)

    mean_lo = jnp.where(first, s2 / cnt(2), s4 / cnt(4))
    mean_hi = jnp.where(first, s8 / cnt(8), s16 / cnt(16))
    pooled = jnp.concatenate([mean_lo, mean_hi], axis=-1) - p_u
    y_c = jnp.dot(pooled.reshape(rows, D_POOL).astype(_BF16), poolw_ref[...],
                  preferred_element_type=_F32).reshape(nb, tm, D_POOL)
    yc_ref[...] = (y_c * pscale_ref[...] * _silu(p_z)).astype(_BF16)
    last_pool = extu_ref[:, POOL_HALO + tm - POOL_CTX:POOL_HALO + tm, :]
    newpool_ref[...] = last_pool
    if n_tiles > 1:
        extu_ref[:, POOL_HALO - POOL_CTX:POOL_HALO, :] = last_pool

    q = pa[:, 0:D_ATTN].reshape(nb, tm, D_ATTN)
    k = pa[:, D_ATTN:2 * D_ATTN].reshape(nb, tm, D_ATTN)
    v = pa[:, 2 * D_ATTN:3 * D_ATTN].reshape(nb, tm, D_ATTN)
    b_z = pa[:, 3 * D_ATTN:4 * D_ATTN].reshape(nb, tm, D_ATTN)
    qb_ref[...] = (q * q_scale).astype(_BF16)
    if kv_transposed:
        k_ref[0, 0] = pa[:, D_ATTN:2 * D_ATTN].T
        v_ref[0, 0] = pa[:, 2 * D_ATTN:3 * D_ATTN].T
    else:
        k_ref[...] = k
        v_ref[...] = v
    kb_ref[...] = k.astype(_BF16)
    vb_ref[...] = v.astype(_BF16)
    gateb_ref[...] = _silu(b_z)


def _inproj_call(x, ada3, norm_w, w_in_bf, conv_prev, pool_prev, conv_w, pool_w_bd, pool_scale,
                 *, layer, nb, tm, pos0, q_scale, kv_stack=None):
    n_seq, t, _ = x.shape
    n_tiles = t // tm
    grid = (n_seq // nb, n_tiles)
    kv_transposed = kv_stack is not None
    aliased = []
    if kv_transposed:
        layer, depth, k_buf, v_buf = kv_stack
        assert nb == 1
        aliased = [] if k_buf is None else [k_buf, v_buf]
        kv_shape = jax.ShapeDtypeStruct((depth, n_seq, D_ATTN, t), _F32)
        kv_spec = pl.BlockSpec((1, 1, D_ATTN, tm), lambda b, i: (layer, b, 0, i))
    else:
        kv_shape = jax.ShapeDtypeStruct((n_seq, t, D_ATTN), _F32)
        kv_spec = pl.BlockSpec((nb, tm, D_ATTN), lambda b, i: (b, i, 0))
    row_spec = lambda width: pl.BlockSpec((nb, tm, width), lambda b, i: (b, i, 0))
    ada_spec = lambda j: pl.BlockSpec((nb, 1, D_MODEL), lambda b, i, j=j: (b, 0, j))
    const2 = lambda shape: pl.BlockSpec(shape, lambda b, i: (0, 0))
    state_spec = lambda r, c: pl.BlockSpec((nb, r, c), lambda b, i: (b, 0, 0))
    sds = jax.ShapeDtypeStruct
    out_shape = (
        sds((n_seq, t, D_CONV), _BF16), sds((n_seq, t, D_POOL), _BF16),
        sds((n_seq, t, D_ATTN), _F32),
        sds((n_seq, t, D_ATTN), _BF16), sds((n_seq, t, D_ATTN), _BF16), sds((n_seq, t, D_ATTN), _BF16),
        kv_shape, kv_shape,
        sds((n_seq, CONV_WIDTH - 1, D_CONV), _F32), sds((n_seq, POOL_CTX, D_POOL), _F32),
    )
    out_specs = (
        row_spec(D_CONV), row_spec(D_POOL), row_spec(D_ATTN),
        row_spec(D_ATTN), row_spec(D_ATTN), row_spec(D_ATTN),
        kv_spec, kv_spec,
        state_spec(CONV_WIDTH - 1, D_CONV), state_spec(POOL_CTX, D_POOL),
    )
    n_in = 10
    return pl.pallas_call(
        functools.partial(_inproj_kernel, nb=nb, tm=tm, n_tiles=n_tiles, pos0=pos0, q_scale=q_scale,
                          kv_transposed=kv_transposed, n_aliased=len(aliased)),
        grid=grid,
        in_specs=[
            row_spec(D_MODEL), ada_spec(0), ada_spec(1),
            const2((1, D_MODEL)), pl.BlockSpec((1, D_MODEL, D_IN), lambda b, i: (layer, 0, 0)),
            state_spec(CONV_WIDTH - 1, D_CONV), state_spec(POOL_CTX, D_POOL),
            const2((CONV_WIDTH, D_CONV)), const2((D_POOL, D_POOL)), const2((1, D_POOL)),
        ] + [pl.BlockSpec(memory_space=pl.ANY)] * len(aliased),
        input_output_aliases={n_in + j: 6 + j for j in range(len(aliased))},
        out_specs=out_specs,
        out_shape=out_shape,
        scratch_shapes=[
            pltpu.VMEM((nb, CONV_HALO + tm, D_CONV), _F32),
            pltpu.VMEM((nb, POOL_HALO + tm, D_POOL), _F32),
        ],
        compiler_params=pltpu.CompilerParams(
            dimension_semantics=("arbitrary", "arbitrary"),
            vmem_limit_bytes=VMEM_LIMIT),
        name="inproj",
    )(x, ada3, ada3, norm_w.reshape(1, D_MODEL), w_in_bf, conv_prev, pool_prev,
      conv_w, pool_w_bd, pool_scale.reshape(1, D_POOL), *aliased)


def _prompt_attn_kernel(bias_ref, q_ref, k_ref, v_ref, gate_ref, tri_ref, o_ref,
                        carry_ref, acc_ref, zha_ref, zhb_ref, *, tq, n_heads):
    g = pl.program_id(1)
    qi = pl.program_id(2)
    n_pairs = n_heads // 2
    lane = lax.broadcasted_iota(jnp.int32, (tq, LANES), 1)
    low = lane < HEAD_DIM
    def bias_lanes(h):
        b = jnp.full((tq, LANES), 0.5 * bias_ref[g * n_heads + h], _F32)
        hi = b.astype(_BF16).astype(_F32)
        lo = (b - hi).astype(_BF16).astype(_F32)
        return jnp.where(lane == 0, hi, jnp.where(lane == 1, lo, 0.0)).astype(_BF16)

    qm = []
    for hp in range(n_pairs):
        q2 = q_ref[0, :, hp * LANES:(hp + 1) * LANES]
        zero = jnp.zeros_like(q2)
        qm += [jnp.concatenate([jnp.where(low, q2, zero), bias_lanes(2 * hp)], axis=1),
               jnp.concatenate([jnp.where(low, zero, q2), bias_lanes(2 * hp + 1)], axis=1)]
    key_ones = jnp.ones((tq, LANES), _BF16)
    tri = tri_ref[...]
    row = lax.broadcasted_iota(jnp.int32, (tq, tq), 0)
    col = lax.broadcasted_iota(jnp.int32, (tq, tq), 1)
    below_diag = col < row

    pair_lanes = lambda h: slice((h // 2) * LANES, (h // 2 + 1) * LANES)

    def half_scores(kb):
        start = pl.multiple_of(kb * tq, tq)
        out = []
        for h in range(n_heads):
            k2 = jnp.concatenate([k_ref[0, pl.ds(start, tq), pair_lanes(h)], key_ones], axis=1)
            out.append(lax.dot_general(qm[h], k2, _NT, preferred_element_type=_F32))
        return tuple(out)

    def block(kb, zhs, diagonal):
        start = pl.multiple_of(kb * tq, tq)
        sps, totals, logsigs, ws = [], [], [], []
        for h in range(n_heads):
            zh = zhs[h]
            ah = jnp.abs(zh)
            l1p = jnp.log(1.0 + jnp.exp2(ah * (-2.0 * LOG2E)))
            sp = (zh + ah) + l1p
            if diagonal:
                sp = jnp.where(below_diag, sp, 0.0)
            sps.append(sp.astype(_BF16))
            totals.append(jnp.sum(sp, axis=1, keepdims=True))
            logsigs.append((zh - ah) - l1p)
        for h in range(n_heads):
            within = jnp.dot(sps[h], tri, preferred_element_type=_F32)
            w = jnp.exp(logsigs[h] + within)
            if diagonal:
                w = jnp.where(below_diag, w, 0.0)
            ws.append(w.astype(_BF16))
        for h in range(n_heads):
            v2 = v_ref[0, pl.ds(start, tq), pair_lanes(h)]
            pv = jnp.dot(ws[h], v2, preferred_element_type=_F32)
            if diagonal:
                acc_ref[h] = pv
                carry_ref[h] = totals[h]
            else:
                carry = carry_ref[h]
                acc_ref[h] = acc_ref[h] + pv * jnp.exp2(carry * (-LOG2E))
                carry_ref[h] = carry + totals[h]

    def stash(ref, kb):
        for h, zh in enumerate(half_scores(jnp.maximum(kb, 0))):
            ref[h] = zh

    def fetch(ref):
        return tuple(ref[h] for h in range(n_heads))

    stash(zha_ref, qi - 1)
    block(qi, half_scores(qi), True)

    def body(pair, c):
        kb = qi - 1 - 2 * pair
        stash(zhb_ref, kb - 1)
        block(kb, fetch(zha_ref), False)

        @pl.when(kb >= 1)
        def _():
            stash(zha_ref, kb - 2)
            block(kb - 1, fetch(zhb_ref), False)

        return c

    lax.fori_loop(0, (qi + 1) // 2, body, 0)
    for hp in range(n_pairs):
        o = jnp.where(low, acc_ref[2 * hp], acc_ref[2 * hp + 1])
        lanes = slice(hp * LANES, (hp + 1) * LANES)
        o_ref[0, :, lanes] = (o * gate_ref[0, :, lanes]).astype(_BF16)


def _prompt_attn_call(sb_bias, q_bf, k_bf, v_bf, gate_b, tri):
    n_seq, t, _ = q_bf.shape
    tq = Q_TILE
    n_heads = ATTN_HEADS_PER_STEP
    width = n_heads * HEAD_DIM
    grid = (n_seq, D_ATTN // width, t // tq)
    tile_spec = pl.BlockSpec((1, tq, width), lambda b, g, qi: (b, qi, g))
    seq_spec = pl.BlockSpec((1, t, width), lambda b, g, qi: (b, 0, g))
    return pl.pallas_call(
        functools.partial(_prompt_attn_kernel, tq=tq, n_heads=n_heads),
        grid=grid,
        in_specs=[
            pl.BlockSpec(memory_space=pltpu.SMEM),
            tile_spec, seq_spec, seq_spec, tile_spec,
            pl.BlockSpec((tq, tq), lambda b, g, qi: (0, 0)),
        ],
        out_specs=tile_spec,
        out_shape=jax.ShapeDtypeStruct((n_seq, t, D_ATTN), _BF16),
        scratch_shapes=[
            pltpu.VMEM((n_heads, tq, 1), _F32),
            pltpu.VMEM((n_heads, tq, LANES), _F32),
            pltpu.VMEM((n_heads, tq, tq), _F32),
            pltpu.VMEM((n_heads, tq, tq), _F32),
        ],
        compiler_params=pltpu.CompilerParams(
            dimension_semantics=("arbitrary", "arbitrary", "arbitrary"),
            vmem_limit_bytes=VMEM_LIMIT),
        name="prompt_attn",
    )(sb_bias, q_bf, k_bf, v_bf, gate_b, tri)


def _sample_attn_kernel(pt_ref, q_ref, bias_ref, knew_ref, vnew_ref, *rest, n_g, page):
    del pt_ref
    k_refs = rest[:n_g]
    v_refs = rest[n_g:2 * n_g]
    tri_ref, o_ref, acc_ref, carry_ref = rest[2 * n_g:]
    j = pl.program_id(1)
    nq = SUBLANES
    rows = N_HEADS * nq
    q_rep = jnp.concatenate([q_ref[0].astype(_F32)] * N_HEADS, axis=0)
    row_head = lax.broadcasted_iota(jnp.int32, (rows, D_ATTN), 0) // nq
    col_head = lax.broadcasted_iota(jnp.int32, (rows, D_ATTN), 1) // HEAD_DIM
    qbd = jnp.where(row_head == col_head, q_rep, 0.0).astype(_BF16)
    bias = bias_ref[0]
    tri = tri_ref[...]

    @pl.when(j == 0)
    def _():
        pad = jnp.zeros((page - nq, D_ATTN), _F32)
        knew = jnp.concatenate([knew_ref[0], pad], axis=0).astype(_BF16)
        vnew = jnp.concatenate([vnew_ref[0], pad], axis=0).astype(_BF16)
        z = lax.dot_general(qbd, knew, _NT, preferred_element_type=_F32) + bias
        key = lax.broadcasted_iota(jnp.int32, (rows, page), 1)
        qry = lax.broadcasted_iota(jnp.int32, (rows, page), 0) & (nq - 1)
        valid = key < qry
        sp = _softplus(z)
        r = jnp.dot(jnp.where(valid, sp, 0.0).astype(_BF16), tri, preferred_element_type=_F32)
        w = jnp.where(valid, jnp.exp(z - sp + r[:, 0:page]), 0.0)
        acc_ref[...] = jnp.dot(w.astype(_BF16), vnew, preferred_element_type=_F32)
        carry_ref[...] = r[:, page:2 * page]

    zs, sps = [], []
    for g in range(n_g):
        kt = k_refs[g][0, 0].reshape(D_ATTN, page).astype(_BF16)
        z = jnp.dot(qbd, kt, preferred_element_type=_F32) + bias
        zs.append(z)
        sps.append(_softplus(z))
    r = jnp.dot(jnp.concatenate(sps, axis=0).astype(_BF16), tri, preferred_element_type=_F32)
    run = carry_ref[...]
    pv = None
    for g in range(n_g):
        blk = slice(g * rows, (g + 1) * rows)
        w = jnp.exp(zs[g] - sps[g] + (r[blk, 0:page] + run))
        run = run + r[blk, page:2 * page]
        vt = v_refs[g][0, 0].reshape(D_ATTN, page).astype(_BF16)
        c = lax.dot_general(w.astype(_BF16), vt, _NT, preferred_element_type=_F32)
        pv = c if pv is None else pv + c
    carry_ref[...] = run
    acc_ref[...] = acc_ref[...] + pv

    @pl.when(j == pl.num_programs(1) - 1)
    def _():
        lane_head = lax.broadcasted_iota(jnp.int32, (nq, D_ATTN), 1) // HEAD_DIM
        out = jnp.zeros((nq, D_ATTN), _F32)
        for h in range(N_HEADS):
            out = jnp.where(lane_head == h, acc_ref[h * nq:(h + 1) * nq, :], out)
        o_ref[0] = out


def _sample_attn_call(layer, page_table, q_bf, bias_rows, k_new, v_new, cache_kt, cache_vt, tri):
    n_seq, n_pages = page_table.shape
    page = cache_kt.shape[-1]
    n_g = PAGES_PER_STEP
    rows = SUBLANES * N_HEADS
    seq_spec = lambda r, c: pl.BlockSpec((1, r, c), lambda b, j, pt: (b, 0, 0))
    const2 = lambda shape: pl.BlockSpec(shape, lambda b, j, pt: (0, 0))

    def page_spec(g):
        return pl.BlockSpec(
            (1, 1, N_HEADS, HEAD_DIM, page),
            lambda b, j, pt, g=g: (layer, pt[b, n_pages - 1 - (j * n_g + g)], 0, 0, 0))

    page_specs = [page_spec(g) for g in range(n_g)]
    grid_spec = pltpu.PrefetchScalarGridSpec(
        num_scalar_prefetch=1,
        grid=(n_seq, n_pages // n_g),
        in_specs=[seq_spec(SUBLANES, D_ATTN),
                  pl.BlockSpec((1, rows, page), lambda b, j, pt: (layer, 0, 0)),
                  seq_spec(SUBLANES, D_ATTN), seq_spec(SUBLANES, D_ATTN)]
                 + page_specs + page_specs + [const2(tri.shape)],
        out_specs=seq_spec(SUBLANES, D_ATTN),
        scratch_shapes=[pltpu.VMEM((rows, D_ATTN), _F32), pltpu.VMEM((rows, page), _F32)],
    )
    return pl.pallas_call(
        functools.partial(_sample_attn_kernel, n_g=n_g, page=page),
        grid_spec=grid_spec,
        out_shape=jax.ShapeDtypeStruct((n_seq, SUBLANES, D_ATTN), _F32),
        compiler_params=pltpu.CompilerParams(
            dimension_semantics=("arbitrary", "arbitrary"),
            vmem_limit_bytes=VMEM_LIMIT),
        name="sample_attn",
    )(page_table, q_bf, bias_rows, k_new, v_new, *([cache_kt] * n_g), *([cache_vt] * n_g), tri)


def _outproj_kernel(*refs, nb, tm, gate_inside, final):
    if gate_inside:
        x_ref, gate_ref, ya_ref, yb_ref, gb_ref, yc_ref, w_ref = refs[:7]
        rest = refs[7:]
    else:
        x_ref, gate_ref, ya_ref, yb_ref, yc_ref, w_ref = refs[:6]
        rest = refs[6:]
    rows = nb * tm
    if gate_inside:
        y_b = (yb_ref[...] * gb_ref[...]).astype(_BF16)
    else:
        y_b = yb_ref[...]
    out = (jnp.dot(ya_ref[...].reshape(rows, D_CONV), w_ref[0, 0:D_CONV, :], preferred_element_type=_F32)
           + jnp.dot(y_b.reshape(rows, D_ATTN), w_ref[0, D_CONV:D_CONV + D_ATTN, :],
                     preferred_element_type=_F32)
           + jnp.dot(yc_ref[...].reshape(rows, D_POOL), w_ref[0, D_CONV + D_ATTN:, :],
                     preferred_element_type=_F32))
    xn = x_ref[...] + gate_ref[...] * out.reshape(nb, tm, D_MODEL)
    if final:
        fw_ref, y_ref = rest
        ms = jnp.mean(xn * xn, axis=-1, keepdims=True)
        y_ref[...] = xn * lax.rsqrt(ms + RMS_EPS) * fw_ref[...]
    else:
        (y_ref,) = rest
        y_ref[...] = xn


def _outproj_call(x, ada3, y_a, y_b, gate_b, y_c, w_out_bf, final_w, *, layer, nb, tm):
    n_seq, t, _ = x.shape
    grid = (n_seq // nb, t // tm)
    row_spec = lambda width: pl.BlockSpec((nb, tm, width), lambda b, i: (b, i, 0))
    gate_inside = gate_b is not None
    final = final_w is not None
    args = [x, ada3, y_a, y_b]
    in_specs = [row_spec(D_MODEL), pl.BlockSpec((nb, 1, D_MODEL), lambda b, i: (b, 0, 2)),
                row_spec(D_CONV), row_spec(D_ATTN)]
    if gate_inside:
        args.append(gate_b)
        in_specs.append(row_spec(D_ATTN))
    args += [y_c, w_out_bf]
    in_specs += [row_spec(D_POOL), pl.BlockSpec((1, D_MODEL, D_MODEL), lambda b, i: (layer, 0, 0))]
    if final:
        args.append(final_w.reshape(1, D_MODEL))
        in_specs.append(pl.BlockSpec((1, D_MODEL), lambda b, i: (0, 0)))
    return pl.pallas_call(
        functools.partial(_outproj_kernel, nb=nb, tm=tm, gate_inside=gate_inside, final=final),
        grid=grid,
        in_specs=in_specs,
        out_specs=row_spec(D_MODEL),
        out_shape=jax.ShapeDtypeStruct(x.shape, _F32),
        compiler_params=pltpu.CompilerParams(
            dimension_semantics=("arbitrary", "arbitrary"),
            vmem_limit_bytes=VMEM_LIMIT),
        name="outproj",
    )(*args)


def _suffix_matrix(n):
    r = jnp.arange(n)
    tri = jnp.where(r[:, None] > r[None, :], -1.0, 0.0)
    return jnp.concatenate([tri, -jnp.ones((n, n))], axis=1).astype(_BF16)


def kernel(x_prompt, x_sample, cache_k, cache_v, state_conv, state_pool, page_table, c_prompt, c_sample,
           norm_w, w_ada, b_ada, w_in, sb_bias, conv_w, pool_w, pool_scale, w_out, final_norm_w):
    depth = norm_w.shape[0]
    n_prompt, seq, _ = x_prompt.shape
    n_sample, dec_seq, _ = x_sample.shape
    n_pages = page_table.shape[1]
    page = cache_k.shape[2]
    past_len = n_pages * page
    assert dec_seq == SUBLANES and seq % ROW_TILE == 0 and seq % Q_TILE == 0
    assert n_pages % PAGES_PER_STEP == 0 and page == LANES

    w_in_bf = w_in.astype(_BF16)
    w_out_bf = w_out.astype(_BF16)
    eye = jnp.eye(N_HEADS, dtype=_F32)
    pool_w_bd = jnp.einsum('lgcd,gh->lgchd', pool_w, eye[:len(POOL_WINDOWS), :len(POOL_WINDOWS)]).reshape(
        depth, D_POOL, D_POOL).astype(_BF16)
    cache_kt = cache_k.transpose(0, 1, 3, 4, 2)
    cache_vt = cache_v.transpose(0, 1, 3, 4, 2)

    ada_p, ada_s = _ada_call(c_prompt, c_sample, w_ada, b_ada)
    ada_p = ada_p.reshape(depth, n_prompt, 1, 3 * D_MODEL)
    ada_s = ada_s.reshape(depth, n_sample, 1, 3 * D_MODEL)

    tri_prompt = _suffix_matrix(Q_TILE)[:, :Q_TILE]
    tri_page = _suffix_matrix(page)
    head_of_row = jnp.arange(N_HEADS * dec_seq) // dec_seq
    bias_rows = jnp.broadcast_to(sb_bias[:, head_of_row, None], (depth, N_HEADS * dec_seq, page))

    zero_conv = jnp.zeros((n_prompt, CONV_WIDTH - 1, D_CONV), _F32)
    zero_pool = jnp.zeros((n_prompt, POOL_CTX, D_POOL), _F32)

    xp, xs = x_prompt, x_sample
    outs = [[] for _ in range(6)]
    kp_t = vp_t = None
    for l in range(depth):
        last = l == depth - 1
        fw = final_norm_w if last else None

        ya, yc, gate_b, qb, kb, vb, kp_t, vp_t, cp, pp = _inproj_call(
            xp, ada_p[l], norm_w[l], w_in_bf, zero_conv, zero_pool, conv_w[l], pool_w_bd[l],
            pool_scale[l], nb=1, tm=ROW_TILE, pos0=0, q_scale=0.5 * HEAD_DIM ** -0.5,
            layer=l, kv_stack=(l, depth, kp_t, vp_t))
        yb = _prompt_attn_call(sb_bias[l], qb, kb, vb, gate_b, tri_prompt)
        xp = _outproj_call(xp, ada_p[l], ya, yb, None, yc, w_out_bf, fw, layer=l, nb=1, tm=ROW_TILE)
        for lst, val in zip(outs[:2], (cp, pp)):
            lst.append(val)

        ya, yc, gate_b, qb, _, _, k, v, cs, ps = _inproj_call(
            xs, ada_s[l], norm_w[l], w_in_bf, state_conv[l], state_pool[l], conv_w[l], pool_w_bd[l],
            pool_scale[l], nb=n_sample, tm=dec_seq, pos0=past_len, q_scale=HEAD_DIM ** -0.5, layer=l)
        o_s = _sample_attn_call(l, page_table, qb, bias_rows, k, v, cache_kt, cache_vt, tri_page)
        xs = _outproj_call(xs, ada_s[l], ya, o_s, gate_b, yc, w_out_bf, fw, layer=l, nb=n_sample, tm=dec_seq)
        for lst, val in zip(outs[2:], (k, v, cs, ps)):
            lst.append(val)

    cpo, ppo, ks, vs, cso, pso = [jnp.stack(o) for o in outs]
    heads = lambda a: a.reshape(a.shape[:-1] + (N_HEADS, HEAD_DIM))
    to_rows = lambda a: a.reshape(depth, n_prompt, N_HEADS, HEAD_DIM, seq).transpose(0, 1, 4, 2, 3)
    return (xp, xs, to_rows(kp_t), to_rows(vp_t), cpo, ppo, heads(ks), heads(vs), cso, pso)
```

```python
import functools

import jax
import jax.numpy as jnp
from jax import lax
from jax.experimental import pallas as pl
from jax.experimental.pallas import tpu as pltpu

D_MODEL = 1024
HEAD_DIM = 64
D_ATTN = 512
N_HEADS = 8
D_CONV = 256
D_POOL = 256
CONV_WIDTH = 3
POOL_WINDOWS = (2, 4, 8, 16)
POOL_GROUP = 64
POOL_CTX = 15
RMS_EPS = 1e-6
LOG2E = 1.4426950408889634
D_IN = 4 * D_CONV + 4 * D_ATTN + 2 * D_POOL
ATTN_COL0 = 4 * D_CONV
POOL_COL0 = ATTN_COL0 + 4 * D_ATTN

LANES = 128
SUBLANES = 8
CONV_HALO = SUBLANES
POOL_HALO = 16
ROW_TILE = 512
Q_TILE = 256
ATTN_HEADS_PER_STEP = 8
PAGES_PER_STEP = 32
VMEM_LIMIT = 52 * 1024 * 1024

_BF16 = jnp.bfloat16
_F32 = jnp.float32
_NT = (((1,), (1,)), ((), ()))


def _silu(x):
    return x * jax.nn.sigmoid(x)


def _softplus(z):
    return jnp.maximum(z, 0.0) + jnp.log(1.0 + jnp.exp2(jnp.abs(z) * (-LOG2E)))


def _ada_kernel(cp_ref, cs_ref, w_ref, b_ref, op_ref, os_ref):
    w = w_ref[0].astype(_BF16)
    for c_ref, o_ref in ((cp_ref, op_ref), (cs_ref, os_ref)):
        a = _silu(c_ref[...]).astype(_BF16)
        o_ref[0] = jnp.dot(a, w, preferred_element_type=_F32) + b_ref[0]


def _ada_call(c_prompt, c_sample, w_ada, b_ada):
    depth = w_ada.shape[0]
    n_p, n_s = c_prompt.shape[0], c_sample.shape[0]
    out_spec = lambda n: pl.BlockSpec((1, n, D_MODEL), lambda l, j: (l, 0, j))
    return pl.pallas_call(
        _ada_kernel,
        grid=(depth, 3),
        in_specs=[
            pl.BlockSpec((n_p, D_MODEL), lambda l, j: (0, 0)),
            pl.BlockSpec((n_s, D_MODEL), lambda l, j: (0, 0)),
            pl.BlockSpec((1, D_MODEL, D_MODEL), lambda l, j: (l, 0, j)),
            pl.BlockSpec((1, 1, D_MODEL), lambda l, j: (l, 0, j)),
        ],
        out_specs=(out_spec(n_p), out_spec(n_s)),
        out_shape=(jax.ShapeDtypeStruct((depth, n_p, 3 * D_MODEL), _F32),
                   jax.ShapeDtypeStruct((depth, n_s, 3 * D_MODEL), _F32)),
        compiler_params=pltpu.CompilerParams(
            dimension_semantics=("arbitrary", "arbitrary"),
            vmem_limit_bytes=VMEM_LIMIT),
        name="ada",
    )(c_prompt, c_sample, w_ada, b_ada.reshape(depth, 1, 3 * D_MODEL))


def _inproj_kernel(x_ref, shift_ref, scale_ref, normw_ref, w_ref, convprev_ref, poolprev_ref,
                   convw_ref, poolw_ref, pscale_ref, *rest,
                   nb, tm, n_tiles, pos0, q_scale, kv_transposed, n_aliased):
    (ya_ref, yc_ref, gateb_ref, qb_ref, kb_ref, vb_ref, k_ref, v_ref,
     newconv_ref, newpool_ref, extc_ref, extu_ref) = rest[n_aliased:]
    i = pl.program_id(1)
    rows = nb * tm

    @pl.when(i == 0)
    def _():
        extc_ref[:, CONV_HALO - 2:CONV_HALO, :] = convprev_ref[...]
        extu_ref[:, POOL_HALO - POOL_CTX:POOL_HALO, :] = poolprev_ref[...]

    def normed(seqs, rws):
        x = x_ref[seqs, rws, :]
        ms = jnp.mean(x * x, axis=-1, keepdims=True)
        xn = x * lax.rsqrt(ms + RMS_EPS) * normw_ref[...]
        h = xn * (1.0 + scale_ref[seqs]) + shift_ref[seqs]
        return h.reshape(rows // 2, D_MODEL).astype(_BF16)

    if nb == 1:
        halves = [(slice(0, 1), slice(0, tm // 2)), (slice(0, 1), slice(tm // 2, tm))]
    else:
        halves = [(slice(0, nb // 2), slice(0, tm)), (slice(nb // 2, nb), slice(0, tm))]
    h_top = normed(*halves[0])
    pc_top = jnp.dot(h_top, w_ref[0, :, 0:ATTN_COL0], preferred_element_type=_F32)
    h_bot = normed(*halves[1])
    pc_bot = jnp.dot(h_bot, w_ref[0, :, 0:ATTN_COL0], preferred_element_type=_F32)
    h2 = jnp.concatenate([h_top, h_bot], axis=0)
    pc = jnp.concatenate([pc_top, pc_bot], axis=0)
    pp = jnp.dot(h2, w_ref[0, :, POOL_COL0:D_IN], preferred_element_type=_F32)
    pa = jnp.dot(h2, w_ref[0, :, ATTN_COL0:POOL_COL0], preferred_element_type=_F32)

    a_b = pc[:, 0:D_CONV].reshape(nb, tm, D_CONV)
    a_c = pc[:, D_CONV:2 * D_CONV]
    a_h = pc[:, 2 * D_CONV:3 * D_CONV]
    a_z = pc[:, 3 * D_CONV:4 * D_CONV].reshape(nb, tm, D_CONV)
    conv_in = (a_c * a_h).reshape(nb, tm, D_CONV)
    extc_ref[:, CONV_HALO:CONV_HALO + tm, :] = conv_in
    cw = convw_ref[...]
    conv_out = (extc_ref[:, CONV_HALO - 2:CONV_HALO - 2 + tm, :] * cw[0:1]
                + extc_ref[:, CONV_HALO - 1:CONV_HALO - 1 + tm, :] * cw[1:2]
                + conv_in * cw[2:3])
    ya_ref[...] = (a_b * conv_out * _silu(a_z)).astype(_BF16)
    last_conv = extc_ref[:, CONV_HALO + tm - 2:CONV_HALO + tm, :]
    newconv_ref[...] = last_conv
    if n_tiles > 1:
        extc_ref[:, CONV_HALO - 2:CONV_HALO, :] = last_conv

    p_u = pp[:, 0:D_POOL].reshape(nb, tm, D_POOL)
    p_z = pp[:, D_POOL:2 * D_POOL].reshape(nb, tm, D_POOL)
    extu_ref[:, POOL_HALO:POOL_HALO + tm, :] = p_u

    def win_sum(lo, hi, lane0):
        acc = extu_ref[:, POOL_HALO - lo:POOL_HALO - lo + tm, lane0:lane0 + LANES]
        for kk in range(lo + 1, hi):
            acc = acc + extu_ref[:, POOL_HALO - kk:POOL_HALO - kk + tm, lane0:lane0 + LANES]
        return acc

    s2 = win_sum(0, 2, 0)
    s4 = s2 + win_sum(2, 4, 0)
    s8 = win_sum(0, 8, LANES)
    s16 = s8 + win_sum(8, 16, LANES)
    pos = pos0 + i * tm + lax.broadcasted_iota(jnp.int32, (nb, tm, LANES), 1)
    lane = lax.broadcasted_iota(jnp.int32, (nb, tm, LANES), 2)
    first = lane < POOL_GROUP

    def cnt(win):
        return jnp.minimum(win, pos + 1).astype(_F32)

    mean_lo = jnp.where(first, s2 / cnt(2), s4 / cnt(4))
    mean_hi = jnp.where(first, s8 / cnt(8), s16 / cnt(16))
    pooled = jnp.concatenate([mean_lo, mean_hi], axis=-1) - p_u
    y_c = jnp.dot(pooled.reshape(rows, D_POOL).astype(_BF16), poolw_ref[...],
                  preferred_element_type=_F32).reshape(nb, tm, D_POOL)
    yc_ref[...] = (y_c * pscale_ref[...] * _silu(p_z)).astype(_BF16)
    last_pool = extu_ref[:, POOL_HALO + tm - POOL_CTX:POOL_HALO + tm, :]
    newpool_ref[...] = last_pool
    if n_tiles > 1:
        extu_ref[:, POOL_HALO - POOL_CTX:POOL_HALO, :] = last_pool

    q = pa[:, 0:D_ATTN].reshape(nb, tm, D_ATTN)
    k = pa[:, D_ATTN:2 * D_ATTN].reshape(nb, tm, D_ATTN)
    v = pa[:, 2 * D_ATTN:3 * D_ATTN].reshape(nb, tm, D_ATTN)
    b_z = pa[:, 3 * D_ATTN:4 * D_ATTN].reshape(nb, tm, D_ATTN)
    qb_ref[...] = (q * q_scale).astype(_BF16)
    if kv_transposed:
        k_ref[0, 0] = pa[:, D_ATTN:2 * D_ATTN].T
        v_ref[0, 0] = pa[:, 2 * D_ATTN:3 * D_ATTN].T
    else:
        k_ref[...] = k
        v_ref[...] = v
    kb_ref[...] = k.astype(_BF16)
    vb_ref[...] = v.astype(_BF16)
    gateb_ref[...] = _silu(b_z)


def _inproj_call(x, ada3, norm_w, w_in_bf, conv_prev, pool_prev, conv_w, pool_w_bd, pool_scale,
                 *, layer, nb, tm, pos0, q_scale, kv_stack=None):
    n_seq, t, _ = x.shape
    n_tiles = t // tm
    grid = (n_seq // nb, n_tiles)
    kv_transposed = kv_stack is not None
    aliased = []
    if kv_transposed:
        layer, depth, k_buf, v_buf = kv_stack
        assert nb == 1
        aliased = [] if k_buf is None else [k_buf, v_buf]
        kv_shape = jax.ShapeDtypeStruct((depth, n_seq, D_ATTN, t), _F32)
        kv_spec = pl.BlockSpec((1, 1, D_ATTN, tm), lambda b, i: (layer, b, 0, i))
    else:
        kv_shape = jax.ShapeDtypeStruct((n_seq, t, D_ATTN), _F32)
        kv_spec = pl.BlockSpec((nb, tm, D_ATTN), lambda b, i: (b, i, 0))
    row_spec = lambda width: pl.BlockSpec((nb, tm, width), lambda b, i: (b, i, 0))
    ada_spec = lambda j: pl.BlockSpec((nb, 1, D_MODEL), lambda b, i, j=j: (b, 0, j))
    const2 = lambda shape: pl.BlockSpec(shape, lambda b, i: (0, 0))
    state_spec = lambda r, c: pl.BlockSpec((nb, r, c), lambda b, i: (b, 0, 0))
    sds = jax.ShapeDtypeStruct
    out_shape = (
        sds((n_seq, t, D_CONV), _BF16), sds((n_seq, t, D_POOL), _BF16),
        sds((n_seq, t, D_ATTN), _F32),
        sds((n_seq, t, D_ATTN), _BF16), sds((n_seq, t, D_ATTN), _BF16), sds((n_seq, t, D_ATTN), _BF16),
        kv_shape, kv_shape,
        sds((n_seq, CONV_WIDTH - 1, D_CONV), _F32), sds((n_seq, POOL_CTX, D_POOL), _F32),
    )
    out_specs = (
        row_spec(D_CONV), row_spec(D_POOL), row_spec(D_ATTN),
        row_spec(D_ATTN), row_spec(D_ATTN), row_spec(D_ATTN),
        kv_spec, kv_spec,
        state_spec(CONV_WIDTH - 1, D_CONV), state_spec(POOL_CTX, D_POOL),
    )
    n_in = 10
    return pl.pallas_call(
        functools.partial(_inproj_kernel, nb=nb, tm=tm, n_tiles=n_tiles, pos0=pos0, q_scale=q_scale,
                          kv_transposed=kv_transposed, n_aliased=len(aliased)),
        grid=grid,
        in_specs=[
            row_spec(D_MODEL), ada_spec(0), ada_spec(1),
            const2((1, D_MODEL)), pl.BlockSpec((1, D_MODEL, D_IN), lambda b, i: (layer, 0, 0)),
            state_spec(CONV_WIDTH - 1, D_CONV), state_spec(POOL_CTX, D_POOL),
            const2((CONV_WIDTH, D_CONV)), const2((D_POOL, D_POOL)), const2((1, D_POOL)),
        ] + [pl.BlockSpec(memory_space=pl.ANY)] * len(aliased),
        input_output_aliases={n_in + j: 6 + j for j in range(len(aliased))},
        out_specs=out_specs,
        out_shape=out_shape,
        scratch_shapes=[
            pltpu.VMEM((nb, CONV_HALO + tm, D_CONV), _F32),
            pltpu.VMEM((nb, POOL_HALO + tm, D_POOL), _F32),
        ],
        compiler_params=pltpu.CompilerParams(
            dimension_semantics=("arbitrary", "arbitrary"),
            vmem_limit_bytes=VMEM_LIMIT),
        name="inproj",
    )(x, ada3, ada3, norm_w.reshape(1, D_MODEL), w_in_bf, conv_prev, pool_prev,
      conv_w, pool_w_bd, pool_scale.reshape(1, D_POOL), *aliased)


def _prompt_attn_kernel(bias_ref, q_ref, k_ref, v_ref, gate_ref, tri_ref, o_ref,
                        carry_ref, acc_ref, zha_ref, zhb_ref, *, tq, n_heads):
    g = pl.program_id(1)
    qi = pl.program_id(2)
    n_pairs = n_heads // 2
    lane = lax.broadcasted_iota(jnp.int32, (tq, LANES), 1)
    low = lane < HEAD_DIM
    def bias_lanes(h):
        b = jnp.full((tq, LANES), 0.5 * bias_ref[g * n_heads + h], _F32)
        hi = b.astype(_BF16).astype(_F32)
        lo = (b - hi).astype(_BF16).astype(_F32)
        return jnp.where(lane == 0, hi, jnp.where(lane == 1, lo, 0.0)).astype(_BF16)

    qm = []
    for hp in range(n_pairs):
        q2 = q_ref[0, :, hp * LANES:(hp + 1) * LANES]
        zero = jnp.zeros_like(q2)
        qm += [jnp.concatenate([jnp.where(low, q2, zero), bias_lanes(2 * hp)], axis=1),
               jnp.concatenate([jnp.where(low, zero, q2), bias_lanes(2 * hp + 1)], axis=1)]
    key_ones = jnp.ones((tq, LANES), _BF16)
    tri = tri_ref[...]
    row = lax.broadcasted_iota(jnp.int32, (tq, tq), 0)
    col = lax.broadcasted_iota(jnp.int32, (tq, tq), 1)
    below_diag = col < row

    pair_lanes = lambda h: slice((h // 2) * LANES, (h // 2 + 1) * LANES)

    def half_scores(kb):
        start = pl.multiple_of(kb * tq, tq)
        out = []
        for h in range(n_heads):
            k2 = jnp.concatenate([k_ref[0, pl.ds(start, tq), pair_lanes(h)], key_ones], axis=1)
            out.append(lax.dot_general(qm[h], k2, _NT, preferred_element_type=_F32))
        return tuple(out)

    def block(kb, zhs, diagonal):
        start = pl.multiple_of(kb * tq, tq)
        sps, totals, logsigs, ws = [], [], [], []
        for h in range(n_heads):
            zh = zhs[h]
            ah = jnp.abs(zh)
            l1p = jnp.log(1.0 + jnp.exp2(ah * (-2.0 * LOG2E)))
            sp = (zh + ah) + l1p
            if diagonal:
                sp = jnp.where(below_diag, sp, 0.0)
            sps.append(sp.astype(_BF16))
            totals.append(jnp.sum(sp, axis=1, keepdims=True))
            logsigs.append((zh - ah) - l1p)
        for h in range(n_heads):
            within = jnp.dot(sps[h], tri, preferred_element_type=_F32)
            w = jnp.exp(logsigs[h] + within)
            if diagonal:
                w = jnp.where(below_diag, w, 0.0)
            ws.append(w.astype(_BF16))
        for h in range(n_heads):
            v2 = v_ref[0, pl.ds(start, tq), pair_lanes(h)]
            pv = jnp.dot(ws[h], v2, preferred_element_type=_F32)
            if diagonal:
                acc_ref[h] = pv
                carry_ref[h] = totals[h]
            else:
                carry = carry_ref[h]
                acc_ref[h] = acc_ref[h] + pv * jnp.exp2(carry * (-LOG2E))
                carry_ref[h] = carry + totals[h]

    def stash(ref, kb):
        for h, zh in enumerate(half_scores(jnp.maximum(kb, 0))):
            ref[h] = zh

    stash(zha_ref, qi - 1)
    block(qi, half_scores(qi), True)

    def body(pair, c):
        kb = qi - 1 - 2 * pair
        stash(zhb_ref, kb - 1)
        block(kb, zha_ref, False)

        @pl.when(kb >= 1)
        def _():
            stash(zha_ref, kb - 2)
            block(kb - 1, zhb_ref, False)

        return c

    lax.fori_loop(0, (qi + 1) // 2, body, 0)
    for hp in range(n_pairs):
        o = jnp.where(low, acc_ref[2 * hp], acc_ref[2 * hp + 1])
        lanes = slice(hp * LANES, (hp + 1) * LANES)
        o_ref[0, :, lanes] = (o * gate_ref[0, :, lanes]).astype(_BF16)


def _prompt_attn_call(sb_bias, q_bf, k_bf, v_bf, gate_b, tri):
    n_seq, t, _ = q_bf.shape
    tq = Q_TILE
    n_heads = ATTN_HEADS_PER_STEP
    width = n_heads * HEAD_DIM
    grid = (n_seq, D_ATTN // width, t // tq)
    tile_spec = pl.BlockSpec((1, tq, width), lambda b, g, qi: (b, qi, g))
    seq_spec = pl.BlockSpec((1, t, width), lambda b, g, qi: (b, 0, g))
    return pl.pallas_call(
        functools.partial(_prompt_attn_kernel, tq=tq, n_heads=n_heads),
        grid=grid,
        in_specs=[
            pl.BlockSpec(memory_space=pltpu.SMEM),
            tile_spec, seq_spec, seq_spec, tile_spec,
            pl.BlockSpec((tq, tq), lambda b, g, qi: (0, 0)),
        ],
        out_specs=tile_spec,
        out_shape=jax.ShapeDtypeStruct((n_seq, t, D_ATTN), _BF16),
        scratch_shapes=[
            pltpu.VMEM((n_heads, tq, 1), _F32),
            pltpu.VMEM((n_heads, tq, LANES), _F32),
            pltpu.VMEM((n_heads, tq, tq), _F32),
            pltpu.VMEM((n_heads, tq, tq), _F32),
        ],
        compiler_params=pltpu.CompilerParams(
            dimension_semantics=("arbitrary", "arbitrary", "arbitrary"),
            vmem_limit_bytes=VMEM_LIMIT),
        name="prompt_attn",
    )(sb_bias, q_bf, k_bf, v_bf, gate_b, tri)


def _sample_attn_kernel(pt_ref, q_ref, bias_ref, knew_ref, vnew_ref, *rest, n_g, page):
    del pt_ref
    k_refs = rest[:n_g]
    v_refs = rest[n_g:2 * n_g]
    tri_ref, o_ref, acc_ref, carry_ref = rest[2 * n_g:]
    j = pl.program_id(1)
    nq = SUBLANES
    rows = N_HEADS * nq
    q_rep = jnp.concatenate([q_ref[0].astype(_F32)] * N_HEADS, axis=0)
    row_head = lax.broadcasted_iota(jnp.int32, (rows, D_ATTN), 0) // nq
    col_head = lax.broadcasted_iota(jnp.int32, (rows, D_ATTN), 1) // HEAD_DIM
    qbd = jnp.where(row_head == col_head, q_rep, 0.0).astype(_BF16)
    bias = bias_ref[0]
    tri = tri_ref[...]

    @pl.when(j == 0)
    def _():
        pad = jnp.zeros((page - nq, D_ATTN), _F32)
        knew = jnp.concatenate([knew_ref[0], pad], axis=0).astype(_BF16)
        vnew = jnp.concatenate([vnew_ref[0], pad], axis=0).astype(_BF16)
        z = lax.dot_general(qbd, knew, _NT, preferred_element_type=_F32) + bias
        key = lax.broadcasted_iota(jnp.int32, (rows, page), 1)
        qry = lax.broadcasted_iota(jnp.int32, (rows, page), 0) & (nq - 1)
        valid = key < qry
        sp = _softplus(z)
        r = jnp.dot(jnp.where(valid, sp, 0.0).astype(_BF16), tri, preferred_element_type=_F32)
        w = jnp.where(valid, jnp.exp(z - sp + r[:, 0:page]), 0.0)
        acc_ref[...] = jnp.dot(w.astype(_BF16), vnew, preferred_element_type=_F32)
        carry_ref[...] = r[:, page:2 * page]

    zs, sps = [], []
    for g in range(n_g):
        kt = k_refs[g][0, 0].reshape(D_ATTN, page).astype(_BF16)
        z = jnp.dot(qbd, kt, preferred_element_type=_F32) + bias
        zs.append(z)
        sps.append(_softplus(z))
    r = jnp.dot(jnp.concatenate(sps, axis=0).astype(_BF16), tri, preferred_element_type=_F32)
    run = carry_ref[...]
    pv = None
    for g in range(n_g):
        blk = slice(g * rows, (g + 1) * rows)
        w = jnp.exp(zs[g] - sps[g] + (r[blk, 0:page] + run))
        run = run + r[blk, page:2 * page]
        vt = v_refs[g][0, 0].reshape(D_ATTN, page).astype(_BF16)
        c = lax.dot_general(w.astype(_BF16), vt, _NT, preferred_element_type=_F32)
        pv = c if pv is None else pv + c
    carry_ref[...] = run
    acc_ref[...] = acc_ref[...] + pv

    @pl.when(j == pl.num_programs(1) - 1)
    def _():
        lane_head = lax.broadcasted_iota(jnp.int32, (nq, D_ATTN), 1) // HEAD_DIM
        out = jnp.zeros((nq, D_ATTN), _F32)
        for h in range(N_HEADS):
            out = jnp.where(lane_head == h, acc_ref[h * nq:(h + 1) * nq, :], out)
        o_ref[0] = out


def _sample_attn_call(layer, page_table, q_bf, bias_rows, k_new, v_new, cache_kt, cache_vt, tri):
    n_seq, n_pages = page_table.shape
    page = cache_kt.shape[-1]
    n_g = PAGES_PER_STEP
    rows = SUBLANES * N_HEADS
    seq_spec = lambda r, c: pl.BlockSpec((1, r, c), lambda b, j, pt: (b, 0, 0))
    const2 = lambda shape: pl.BlockSpec(shape, lambda b, j, pt: (0, 0))

    def page_spec(g):
        return pl.BlockSpec(
            (1, 1, N_HEADS, HEAD_DIM, page),
            lambda b, j, pt, g=g: (layer, pt[b, n_pages - 1 - (j * n_g + g)], 0, 0, 0))

    page_specs = [page_spec(g) for g in range(n_g)]
    grid_spec = pltpu.PrefetchScalarGridSpec(
        num_scalar_prefetch=1,
        grid=(n_seq, n_pages // n_g),
        in_specs=[seq_spec(SUBLANES, D_ATTN),
                  pl.BlockSpec((1, rows, page), lambda b, j, pt: (layer, 0, 0)),
                  seq_spec(SUBLANES, D_ATTN), seq_spec(SUBLANES, D_ATTN)]
                 + page_specs + page_specs + [const2(tri.shape)],
        out_specs=seq_spec(SUBLANES, D_ATTN),
        scratch_shapes=[pltpu.VMEM((rows, D_ATTN), _F32), pltpu.VMEM((rows, page), _F32)],
    )
    return pl.pallas_call(
        functools.partial(_sample_attn_kernel, n_g=n_g, page=page),
        grid_spec=grid_spec,
        out_shape=jax.ShapeDtypeStruct((n_seq, SUBLANES, D_ATTN), _F32),
        compiler_params=pltpu.CompilerParams(
            dimension_semantics=("arbitrary", "arbitrary"),
            vmem_limit_bytes=VMEM_LIMIT),
        name="sample_attn",
    )(page_table, q_bf, bias_rows, k_new, v_new, *([cache_kt] * n_g), *([cache_vt] * n_g), tri)


def _outproj_kernel(*refs, nb, tm, gate_inside, final):
    if gate_inside:
        x_ref, gate_ref, ya_ref, yb_ref, gb_ref, yc_ref, w_ref = refs[:7]
        rest = refs[7:]
    else:
        x_ref, gate_ref, ya_ref, yb_ref, yc_ref, w_ref = refs[:6]
        rest = refs[6:]
    rows = nb * tm
    if gate_inside:
        y_b = (yb_ref[...] * gb_ref[...]).astype(_BF16)
    else:
        y_b = yb_ref[...]
    out = (jnp.dot(ya_ref[...].reshape(rows, D_CONV), w_ref[0, 0:D_CONV, :], preferred_element_type=_F32)
           + jnp.dot(y_b.reshape(rows, D_ATTN), w_ref[0, D_CONV:D_CONV + D_ATTN, :],
                     preferred_element_type=_F32)
           + jnp.dot(yc_ref[...].reshape(rows, D_POOL), w_ref[0, D_CONV + D_ATTN:, :],
                     preferred_element_type=_F32))
    xn = x_ref[...] + gate_ref[...] * out.reshape(nb, tm, D_MODEL)
    if final:
        fw_ref, y_ref = rest
        ms = jnp.mean(xn * xn, axis=-1, keepdims=True)
        y_ref[...] = xn * lax.rsqrt(ms + RMS_EPS) * fw_ref[...]
    else:
        (y_ref,) = rest
        y_ref[...] = xn


def _outproj_call(x, ada3, y_a, y_b, gate_b, y_c, w_out_bf, final_w, *, layer, nb, tm):
    n_seq, t, _ = x.shape
    grid = (n_seq // nb, t // tm)
    row_spec = lambda width: pl.BlockSpec((nb, tm, width), lambda b, i: (b, i, 0))
    gate_inside = gate_b is not None
    final = final_w is not None
    args = [x, ada3, y_a, y_b]
    in_specs = [row_spec(D_MODEL), pl.BlockSpec((nb, 1, D_MODEL), lambda b, i: (b, 0, 2)),
                row_spec(D_CONV), row_spec(D_ATTN)]
    if gate_inside:
        args.append(gate_b)
        in_specs.append(row_spec(D_ATTN))
    args += [y_c, w_out_bf]
    in_specs += [row_spec(D_POOL), pl.BlockSpec((1, D_MODEL, D_MODEL), lambda b, i: (layer, 0, 0))]
    if final:
        args.append(final_w.reshape(1, D_MODEL))
        in_specs.append(pl.BlockSpec((1, D_MODEL), lambda b, i: (0, 0)))
    return pl.pallas_call(
        functools.partial(_outproj_kernel, nb=nb, tm=tm, gate_inside=gate_inside, final=final),
        grid=grid,
        in_specs=in_specs,
        out_specs=row_spec(D_MODEL),
        out_shape=jax.ShapeDtypeStruct(x.shape, _F32),
        compiler_params=pltpu.CompilerParams(
            dimension_semantics=("arbitrary", "arbitrary"),
            vmem_limit_bytes=VMEM_LIMIT),
        name="outproj",
    )(*args)


def _suffix_matrix(n):
    r = jnp.arange(n)
    tri = jnp.where(r[:, None] > r[None, :], -1.0, 0.0)
    return jnp.concatenate([tri, -jnp.ones((n, n))], axis=1).astype(_BF16)


def kernel(x_prompt, x_sample, cache_k, cache_v, state_conv, state_pool, page_table, c_prompt, c_sample,
           norm_w, w_ada, b_ada, w_in, sb_bias, conv_w, pool_w, pool_scale, w_out, final_norm_w):
    depth = norm_w.shape[0]
    n_prompt, seq, _ = x_prompt.shape
    n_sample, dec_seq, _ = x_sample.shape
    n_pages = page_table.shape[1]
    page = cache_k.shape[2]
    past_len = n_pages * page
    assert dec_seq == SUBLANES and seq % ROW_TILE == 0 and seq % Q_TILE == 0
    assert n_pages % PAGES_PER_STEP == 0 and page == LANES

    w_in_bf = w_in.astype(_BF16)
    w_out_bf = w_out.astype(_BF16)
    eye = jnp.eye(N_HEADS, dtype=_F32)
    pool_w_bd = jnp.einsum('lgcd,gh->lgchd', pool_w, eye[:len(POOL_WINDOWS), :len(POOL_WINDOWS)]).reshape(
        depth, D_POOL, D_POOL).astype(_BF16)
    cache_kt = cache_k.transpose(0, 1, 3, 4, 2)
    cache_vt = cache_v.transpose(0, 1, 3, 4, 2)

    ada_p, ada_s = _ada_call(c_prompt, c_sample, w_ada, b_ada)
    ada_p = ada_p.reshape(depth, n_prompt, 1, 3 * D_MODEL)
    ada_s = ada_s.reshape(depth, n_sample, 1, 3 * D_MODEL)

    tri_prompt = _suffix_matrix(Q_TILE)[:, :Q_TILE]
    tri_page = _suffix_matrix(page)
    head_of_row = jnp.arange(N_HEADS * dec_seq) // dec_seq
    bias_rows = jnp.broadcast_to(sb_bias[:, head_of_row, None], (depth, N_HEADS * dec_seq, page))

    zero_conv = jnp.zeros((n_prompt, CONV_WIDTH - 1, D_CONV), _F32)
    zero_pool = jnp.zeros((n_prompt, POOL_CTX, D_POOL), _F32)

    xp, xs = x_prompt, x_sample
    outs = [[] for _ in range(6)]
    kp_t = vp_t = None
    for l in range(depth):
        last = l == depth - 1
        fw = final_norm_w if last else None

        ya, yc, gate_b, qb, kb, vb, kp_t, vp_t, cp, pp = _inproj_call(
            xp, ada_p[l], norm_w[l], w_in_bf, zero_conv, zero_pool, conv_w[l], pool_w_bd[l],
            pool_scale[l], nb=1, tm=ROW_TILE, pos0=0, q_scale=0.5 * HEAD_DIM ** -0.5,
            layer=l, kv_stack=(l, depth, kp_t, vp_t))
        yb = _prompt_attn_call(sb_bias[l], qb, kb, vb, gate_b, tri_prompt)
        xp = _outproj_call(xp, ada_p[l], ya, yb, None, yc, w_out_bf, fw, layer=l, nb=1, tm=ROW_TILE)
        for lst, val in zip(outs[:2], (cp, pp)):
            lst.append(val)

        ya, yc, gate_b, qb, _, _, k, v, cs, ps = _inproj_call(
            xs, ada_s[l], norm_w[l], w_in_bf, state_conv[l], state_pool[l], conv_w[l], pool_w_bd[l],
            pool_scale[l], nb=n_sample, tm=dec_seq, pos0=past_len, q_scale=HEAD_DIM ** -0.5, layer=l)
        o_s = _sample_attn_call(l, page_table, qb, bias_rows, k, v, cache_kt, cache_vt, tri_page)
        xs = _outproj_call(xs, ada_s[l], ya, o_s, gate_b, yc, w_out_bf, fw, layer=l, nb=n_sample, tm=dec_seq)
        for lst, val in zip(outs[2:], (k, v, cs, ps)):
            lst.append(val)

    cpo, ppo, ks, vs, cso, pso = [jnp.stack(o) for o in outs]
    heads = lambda a: a.reshape(a.shape[:-1] + (N_HEADS, HEAD_DIM))
    to_rows = lambda a: a.reshape(depth, n_prompt, N_HEADS, HEAD_DIM, seq).transpose(0, 1, 4, 2, 3)
    return (xp, xs, to_rows(kp_t), to_rows(vp_t), cpo, ppo, heads(ks), heads(vs), cso, pso)
```
